```python
import jax, jax.numpy as jnp
from jax import lax
import numpy as np

D_MODEL = 1024
BATCH = 8
SEQ = 2048
DEPTH = 4

HEAD_DIM = 64
GRID_W = 64
Q_BLOCK = 128
EPS = 1e-6
NEG_BIG = -1e30
F_MIN = 1e-6
A_HEADS = 6
A_KV_HEADS = 2
ROPE_THETA = 10000.0
B_HEADS = 4
B_KEY_DIM = 64
B_CHUNK = 64
C_HEADS = 6
C_KV_HEADS = 2
C_BRANCHES = ((128, 1), (512, 4), (2048, 16))
D_FF = 2816
CONV_W = 3

A_W = A_HEADS * HEAD_DIM
B_W = B_HEADS * HEAD_DIM
B_K = B_HEADS * B_KEY_DIM
C_W = C_HEADS * HEAD_DIM
MIX_W = A_W + B_W + C_W
SPLITS = (A_W, A_KV_HEADS * HEAD_DIM, A_KV_HEADS * HEAD_DIM,
          B_K, B_K, B_K, B_W, B_W,
          C_W, C_KV_HEADS * HEAD_DIM, C_KV_HEADS * HEAD_DIM)
IN_W = sum(SPLITS)

kernel_name = "hybrid_parallel_rope_hgrn2_dilated_encoder"


def rms_norm(x, g):
    xf = x.astype(jnp.float32)
    y = xf * lax.rsqrt(jnp.mean(xf * xf, axis=-1, keepdims=True) + EPS)
    return (y * g.astype(jnp.float32)).astype(x.dtype)


def _rope_half(x, cos, sin):
    h = x.shape[-1] // 2
    x1, x2 = x[..., :h], x[..., h:]
    return jnp.concatenate([x1 * cos - x2 * sin, x2 * cos + x1 * sin], axis=-1)


def axial_rope(x):
    S = x.shape[1]
    n_rows = S // GRID_W
    row = jnp.repeat(jnp.arange(n_rows), GRID_W).astype(jnp.float32)
    col = jnp.tile(jnp.arange(GRID_W), n_rows).astype(jnp.float32)
    half = HEAD_DIM // 2
    inv = ROPE_THETA ** (-jnp.arange(0, half, 2, dtype=jnp.float32) / half)
    ang_r = (row[:, None] * inv)[:, None, :]
    ang_c = (col[:, None] * inv)[:, None, :]
    xf = x.astype(jnp.float32)
    xr = _rope_half(xf[..., :half], jnp.cos(ang_r), jnp.sin(ang_r))
    xc = _rope_half(xf[..., half:], jnp.cos(ang_c), jnp.sin(ang_c))
    return jnp.concatenate([xr, xc], axis=-1).astype(x.dtype)


def _to_query_blocks(q, n_kv):
    B_, S, H, D = q.shape
    nb = S // Q_BLOCK
    return q.reshape(B_, nb, Q_BLOCK, n_kv, H // n_kv, D).transpose(1, 0, 3, 4, 2, 5)


def _from_query_blocks(o):
    nb, B_, hk, g, qb, d = o.shape
    return o.transpose(1, 0, 4, 2, 3, 5).reshape(B_, nb * qb, hk * g * d)


def mixer_a(q, k, v, gq, gk):
    q = axial_rope(rms_norm(q, gq))
    k = axial_rope(rms_norm(k, gk))
    qb = _to_query_blocks(q, A_KV_HEADS)
    kt = k.transpose(0, 2, 1, 3)
    vt = v.transpose(0, 2, 1, 3)
    scale = HEAD_DIM ** -0.5

    def block(qi):
        s = jnp.einsum('bhgqd,bhkd->bhgqk', qi, kt).astype(jnp.float32) * scale
        p = jax.nn.softmax(s, axis=-1).astype(vt.dtype)
        return jnp.einsum('bhgqk,bhkd->bhgqd', p, vt)

    return _from_query_blocks(lax.map(block, qb))


def mixer_c(q, k, v, gq, gk):
    S = q.shape[1]
    q = rms_norm(q, gq)
    k = rms_norm(k, gk)
    qb = _to_query_blocks(q, C_KV_HEADS)
    kt = k.transpose(0, 2, 1, 3)
    vt = v.transpose(0, 2, 1, 3)
    nb = S // Q_BLOCK
    starts = jnp.arange(nb) * Q_BLOCK
    scale = HEAD_DIM ** -0.5
    slopes = (2.0 ** (-8.0 * np.arange(1, C_HEADS + 1) / C_HEADS)).astype(np.float32)
    slopes = jnp.asarray(slopes).reshape(C_KV_HEADS, C_HEADS // C_KV_HEADS, 1, 1)
    offsets = [r * np.arange(-(w // (2 * r)), w // (2 * r) + 1) for (w, r) in C_BRANCHES]

    def block(args):
        qi, t0 = args
        t = t0 + jnp.arange(Q_BLOCK)
        lses, outs = [], []
        for off in offsets:
            idx = t[:, None] + jnp.asarray(off)[None, :]
            valid = (idx >= 0) & (idx < S)
            idx = jnp.clip(idx, 0, S - 1)
            kg = kt[:, :, idx, :]
            vg = vt[:, :, idx, :]
            dist = jnp.asarray(np.abs(off).astype(np.float32))
            s = jnp.einsum('bhgqd,bhqkd->bhgqk', qi, kg).astype(jnp.float32) * scale - slopes * dist
            s = jnp.where(valid, s, NEG_BIG)
            lse = jax.nn.logsumexp(s, axis=-1)
            p = jnp.exp(s - lse[..., None]).astype(vg.dtype)
            outs.append(jnp.einsum('bhgqk,bhqkd->bhgqd', p, vg).astype(jnp.float32))
            lses.append(lse)
        w = jax.nn.softmax(jnp.stack(lses, axis=0), axis=0)
        o = jnp.einsum('nbhgq,nbhgqd->bhgqd', w, jnp.stack(outs, axis=0))
        return o.astype(qi.dtype)

    return _from_query_blocks(lax.map(block, (qb, starts)))


def hgrn2_scan(q, k, v, logf):
    B_, S, H, dk = q.shape
    dv = v.shape[-1]
    nc = S // B_CHUNK

    def to_chunks(a):
        return a.astype(jnp.float32).reshape(B_, nc, B_CHUNK, H, a.shape[-1]).transpose(1, 0, 3, 2, 4)

    qc, kc, vc, lc = to_chunks(q), to_chunks(k), to_chunks(v), to_chunks(logf)
    mask = jnp.tril(jnp.ones((B_CHUNK, B_CHUNK), dtype=bool))[:, :, None]

    def step(state, inp):
        q_, k_, v_, l_ = inp
        b = jnp.cumsum(l_, axis=2)
        o_inter = jnp.einsum('bhtk,bhkv->bhtv', q_ * jnp.exp(b), state)
        diff = b[:, :, :, None, :] - b[:, :, None, :, :]
        dec = jnp.where(mask, jnp.exp(jnp.where(mask, diff, 0.0)), 0.0)
        attn = jnp.einsum('bhtk,bhsk,bhtsk->bhts', q_, k_, dec)
        o_intra = jnp.einsum('bhts,bhsv->bhtv', attn, v_)
        b_last = b[:, :, -1:, :]
        new_state = jnp.exp(b_last[:, :, 0, :])[..., None] * state + \
            jnp.einsum('bhsk,bhsv->bhkv', k_ * jnp.exp(b_last - b), v_)
        return new_state, o_inter + o_intra

    state0 = jnp.zeros((B_, H, dk, dv), jnp.float32)
    _, o = lax.scan(step, state0, (qc, kc, vc, lc))
    return o.transpose(1, 0, 3, 2, 4).reshape(B_, S, H, dv).astype(v.dtype)


def mixer_b(q, f_fwd, f_bwd, i, g, lb_fwd, lb_bwd, g_norm):
    def log_forget(fpre, lb):
        lb = lb.reshape(B_HEADS, B_KEY_DIM)
        f = lb + (1.0 - lb) * jax.nn.sigmoid(fpre.astype(jnp.float32))
        return jnp.log(jnp.maximum(f, F_MIN))

    lff = log_forget(f_fwd, lb_fwd)
    lfb = log_forget(f_bwd, lb_bwd)
    o_f = hgrn2_scan(q, -jnp.expm1(lff), i, lff)
    flip = lambda a: jnp.flip(a, axis=1)
    o_b = flip(hgrn2_scan(flip(q), flip(-jnp.expm1(lfb)), flip(i), flip(lfb)))
    o = rms_norm(o_f + o_b, g_norm) * jax.nn.silu(g)
    return o.reshape(o.shape[0], o.shape[1], B_W)


def conv_ffn(h, w_up, conv_w, conv_b, w_down):
    u = h @ w_up
    up = jnp.pad(u, ((0, 0), (1, 1), (0, 0)))
    u = conv_w[0] * up[:, :-2] + conv_w[1] * up[:, 1:-1] + conv_w[2] * up[:, 2:] + conv_b
    a, b = jnp.split(u, 2, axis=-1)
    return (jax.nn.silu(a) * b) @ w_down


def setup_inputs(seed: int = 0) -> dict:
    key = jax.random.key(seed)
    ks = jax.random.split(key, 20)
    nrm = lambda k, shape, s: jax.random.normal(k, shape, jnp.float32) * s
    return {
        "x": nrm(ks[0], (BATCH, SEQ, D_MODEL), 1.0),
        "c": nrm(ks[1], (BATCH, D_MODEL), 1.0),
        "w_ada": nrm(ks[2], (DEPTH, D_MODEL, 6 * D_MODEL), D_MODEL ** -0.5),
        "b_ada": nrm(ks[3], (DEPTH, 6 * D_MODEL), 0.01),
        "norm_g": 1.0 + nrm(ks[4], (DEPTH, 2, D_MODEL), 0.01),
        "w_in": nrm(ks[5], (DEPTH, D_MODEL, IN_W), D_MODEL ** -0.5),
        "a_q_norm": 1.0 + nrm(ks[6], (DEPTH, HEAD_DIM), 0.01),
        "a_k_norm": 1.0 + nrm(ks[7], (DEPTH, HEAD_DIM), 0.01),
        "b_lb": nrm(ks[8], (2, DEPTH, B_K), 0.5),
        "b_out_norm": 1.0 + nrm(ks[9], (DEPTH, HEAD_DIM), 0.01),
        "c_q_norm": 1.0 + nrm(ks[10], (DEPTH, HEAD_DIM), 0.01),
        "c_k_norm": 1.0 + nrm(ks[11], (DEPTH, HEAD_DIM), 0.01),
        "w_out": nrm(ks[12], (DEPTH, MIX_W, D_MODEL), MIX_W ** -0.5),
        "w_up": nrm(ks[13], (DEPTH, D_MODEL, 2 * D_FF), D_MODEL ** -0.5),
        "conv_w": nrm(ks[14], (DEPTH, CONV_W, 2 * D_FF), CONV_W ** -0.5),
        "conv_b": nrm(ks[15], (DEPTH, 2 * D_FF), 0.01),
        "w_down": nrm(ks[16], (DEPTH, D_FF, D_MODEL), D_FF ** -0.5),
    }


def reference(x, c, w_ada, b_ada, norm_g, w_in, a_q_norm, a_k_norm, b_lb, b_out_norm,
              c_q_norm, c_k_norm, w_out, w_up, conv_w, conv_b, w_down):
    B_, S, _ = x.shape
    sm = jax.nn.softmax(b_lb.astype(jnp.float32), axis=1)
    lb_all = jnp.cumsum(sm, axis=1) - sm[:, :1]
    split_idx = np.cumsum(SPLITS)[:-1].tolist()
    for l in range(DEPTH):
        mod = jax.nn.silu(c) @ w_ada[l] + b_ada[l]
        sh1, sc1, g1, sh2, sc2, g2 = jnp.split(mod[:, None, :], 6, axis=-1)
        h = rms_norm(x, norm_g[l, 0]) * (1.0 + sc1) + sh1
        parts = jnp.split(h @ w_in[l], split_idx, axis=-1)
        hd = lambda a, n: a.reshape(B_, S, n, a.shape[-1] // n)
        aq, ak, av, bq, bff, bfb, bi, bg, cq, ck, cv = parts
        o_a = mixer_a(hd(aq, A_HEADS), hd(ak, A_KV_HEADS), hd(av, A_KV_HEADS), a_q_norm[l], a_k_norm[l])
        o_b = mixer_b(hd(bq, B_HEADS), hd(bff, B_HEADS), hd(bfb, B_HEADS), hd(bi, B_HEADS), hd(bg, B_HEADS),
                      lb_all[0, l], lb_all[1, l], b_out_norm[l])
        o_c = mixer_c(hd(cq, C_HEADS), hd(ck, C_KV_HEADS), hd(cv, C_KV_HEADS), c_q_norm[l], c_k_norm[l])
        mix = jnp.concatenate([o_a, o_b, o_c], axis=-1) @ w_out[l]
        x = x + g1 * mix
        h = rms_norm(x, norm_g[l, 1]) * (1.0 + sc2) + sh2
        x = x + g2 * conv_ffn(h, w_up[l], conv_w[l], conv_b[l], w_down[l])
    return x
```

```python
import functools

import numpy as np
import jax
import jax.numpy as jnp
from jax import lax
from jax.experimental import pallas as pl
from jax.experimental.pallas import tpu as pltpu

F32 = jnp.float32
BF16 = jnp.bfloat16

HEAD_DIM = 64
GRID_W = 64
EPS = 1e-6
NEG_BIG = -1e30
F_MIN = 1e-6
ROPE_THETA = 10000.0
A_HEADS, A_KV_HEADS = 6, 2
B_HEADS = 4
C_HEADS, C_KV_HEADS = 6, 2
C_BRANCHES = ((128, 1), (512, 4), (2048, 16))
CONV_W = 3

A_W = A_HEADS * HEAD_DIM
KV_W = A_KV_HEADS * HEAD_DIM
B_W = B_HEADS * HEAD_DIM
C_W = C_HEADS * HEAD_DIM
OFF_AQ, OFF_AK, OFF_AV = 0, A_W, A_W + KV_W
OFF_B = A_W + 2 * KV_W
OFF_CQ = OFF_B + 5 * B_W
OFF_CK, OFF_CV = OFF_CQ + C_W, OFF_CQ + C_W + KV_W

TM_PROJ = 512
TQ_ATTN = 256
HG_CHUNK = 64
HG_BLK = 8
HG_MID = 4
TN_FFN = 256
TN_ADA = 1536
VMEM_LIMIT = 56 * 1024 * 1024


def _cparams(*sem):
    return pltpu.CompilerParams(dimension_semantics=sem, vmem_limit_bytes=VMEM_LIMIT)


def _dot(a, b):
    return jnp.dot(a, b, preferred_element_type=F32)


def _dot_nt(a, b):
    return lax.dot_general(a, b, (((1,), (1,)), ((), ())), preferred_element_type=F32)


def _dot_tn(a, b):
    return lax.dot_general(a, b, (((0,), (0,)), ((), ())), preferred_element_type=F32)


def _split2(x):
    hi = x.astype(BF16)
    lo = (x - hi.astype(F32)).astype(BF16)
    return hi, lo


def _head_mean_sq(x, bd):
    hi, lo = _split2(x * x)
    return (_dot(hi, bd) + _dot(lo, bd)) * (1.0 / HEAD_DIM)


def _silu(x):
    return x * jax.nn.sigmoid(x)


def _ada_kernel(c_ref, w_ref, b_ref, o_ref):
    a = _silu(c_ref[...]).astype(BF16)
    o_ref[...] = _dot(a, w_ref[...].astype(BF16)) + b_ref[...]


def _ada(c, w_ada, b_ada):
    depth, d, n = w_ada.shape
    bsz = c.shape[0]
    return pl.pallas_call(
        _ada_kernel,
        grid=(depth, n // TN_ADA),
        in_specs=[
            pl.BlockSpec((bsz, d), lambda l, j: (0, 0)),
            pl.BlockSpec((None, d, TN_ADA), lambda l, j: (l, 0, j)),
            pl.BlockSpec((None, 1, TN_ADA), lambda l, j: (l, 0, j)),
        ],
        out_specs=pl.BlockSpec((None, bsz, TN_ADA), lambda l, j: (l, 0, j)),
        out_shape=jax.ShapeDtypeStruct((depth, bsz, n), F32),
        compiler_params=_cparams("arbitrary", "arbitrary"),
        name="ada",
    )(c, w_ada, b_ada.reshape(depth, 1, n))


def _lb_kernel(b_ref, o_ref):
    depth = b_ref.shape[1]
    for d in range(2):
        rows = [b_ref[d, l:l + 1, :] for l in range(depth)]
        m = functools.reduce(jnp.maximum, rows)
        e = [jnp.exp(r - m) for r in rows]
        tot = functools.reduce(lambda a, b: a + b, e)
        run = jnp.zeros_like(m)
        for l in range(depth):
            sm = e[l] / tot
            run = run + sm
            o_ref[d, l:l + 1, :] = run - e[0] / tot


def _hgrn_lb(b_lb):
    return pl.pallas_call(
        _lb_kernel,
        out_shape=jax.ShapeDtypeStruct(b_lb.shape, F32),
        name="hgrn_lb",
    )(b_lb)


def _rope(z, cos, sin):
    n = z.shape[-1]
    lane = lax.broadcasted_iota(jnp.int32, z.shape, 1)
    up = pltpu.roll(z, n - 16, 1)
    dn = pltpu.roll(z, 16, 1)
    return z * cos + jnp.where((lane % 32) < 16, up, dn) * sin


def _in_proj_kernel(x_ref, sh_ref, sc_ref, g_ref, w_ref, cos_ref, sin_ref,
                    gaq_ref, gak_ref, gcq_ref, gck_ref, bd_ref,
                    aq_ref, akt_ref, av_ref, bmix_ref, cq_ref, ckt_ref, cv_ref):
    x = x_ref[...]
    y = x * lax.rsqrt(jnp.mean(x * x, axis=-1, keepdims=True) + EPS) * g_ref[...]
    h = (y * (1.0 + sc_ref[...]) + sh_ref[...]).astype(BF16)

    def proj(off, width):
        return _dot(h, w_ref[:, off:off + width])

    def headnorm(z, gain):
        n = z.shape[-1]
        return z * lax.rsqrt(_head_mean_sq(z, bd_ref[:n, :n]) + EPS) * gain

    cos, sin = cos_ref[...], sin_ref[...]
    aq_ref[...] = _rope(headnorm(proj(OFF_AQ, A_W), gaq_ref[...]), cos, sin).astype(BF16)
    ak = _rope(headnorm(proj(OFF_AK, KV_W), gak_ref[...]), cos[:, :KV_W], sin[:, :KV_W])
    akt_ref[...] = ak.T.astype(BF16)
    av_ref[...] = proj(OFF_AV, KV_W).astype(BF16)
    bmix_ref[...] = proj(OFF_B, 5 * B_W)
    cq_ref[...] = headnorm(proj(OFF_CQ, C_W), gcq_ref[...]).astype(BF16)
    ckt_ref[...] = headnorm(proj(OFF_CK, KV_W), gck_ref[...]).T.astype(BF16)
    cv_ref[...] = proj(OFF_CV, KV_W).astype(BF16)


def _in_proj(x2, mod3, g, w, cos, sin, gaq, gak, gcq, gck, bd, seq):
    t, d = x2.shape
    tm = TM_PROJ
    per_seq = seq // tm
    row = lambda i: (i, 0)
    const = lambda i: (0, 0)
    modspec = lambda k: pl.BlockSpec((None, 1, d), lambda i: (i // per_seq, 0, k))
    return pl.pallas_call(
        _in_proj_kernel,
        grid=(t // tm,),
        in_specs=[
            pl.BlockSpec((tm, d), row),
            modspec(0), modspec(1),
            pl.BlockSpec((1, d), const),
            pl.BlockSpec(w.shape, const),
            pl.BlockSpec((tm, A_W), lambda i: (i % per_seq, 0)),
            pl.BlockSpec((tm, A_W), lambda i: (i % per_seq, 0)),
            pl.BlockSpec((1, A_W), const), pl.BlockSpec((1, KV_W), const),
            pl.BlockSpec((1, C_W), const), pl.BlockSpec((1, KV_W), const),
            pl.BlockSpec(bd.shape, const),
        ],
        out_specs=[
            pl.BlockSpec((tm, A_W), row),
            pl.BlockSpec((KV_W, tm), lambda i: (0, i)),
            pl.BlockSpec((tm, KV_W), row),
            pl.BlockSpec((tm, 5 * B_W), row),
            pl.BlockSpec((tm, C_W), row),
            pl.BlockSpec((KV_W, tm), lambda i: (0, i)),
            pl.BlockSpec((tm, KV_W), row),
        ],
        out_shape=[
            jax.ShapeDtypeStruct((t, A_W), BF16),
            jax.ShapeDtypeStruct((KV_W, t), BF16),
            jax.ShapeDtypeStruct((t, KV_W), BF16),
            jax.ShapeDtypeStruct((t, 5 * B_W), F32),
            jax.ShapeDtypeStruct((t, C_W), BF16),
            jax.ShapeDtypeStruct((KV_W, t), BF16),
            jax.ShapeDtypeStruct((t, KV_W), BF16),
        ],
        compiler_params=_cparams("arbitrary"),
        name="in_proj",
    )(x2, mod3, mod3, g, w, cos, sin, gaq, gak, gcq, gck, bd)


def _alibi_slopes():
    return [float(s) for s in (2.0 ** (-8.0 * np.arange(1, C_HEADS + 1) / C_HEADS)).astype(np.float32)]


def _attn_kernel(q_ref, kt_ref, v_ref, o_ref, *, dilated, n_heads, n_kv):
    tq = q_ref.shape[0]
    seq = kt_ref.shape[1]
    group = n_heads // n_kv
    if dilated:
        t0 = pl.program_id(1) * tq
        row = lax.broadcasted_iota(jnp.int32, (tq, seq), 0) + t0
        col = lax.broadcasted_iota(jnp.int32, (tq, seq), 1)
        dist = jnp.abs(col - row)
        count = jnp.zeros((tq, seq), jnp.int32)
        for window, dil in C_BRANCHES:
            hit = (dist <= window // 2) & ((dist & (dil - 1)) == 0)
            count = count + hit.astype(jnp.int32)
        log_mult = jnp.where(count == 3, float(np.log(3.0)),
                             jnp.where(count == 2, float(np.log(2.0)),
                                       jnp.where(count == 1, 0.0, NEG_BIG)))
        distf = dist.astype(F32)
        slopes = _alibi_slopes()
    outs = []
    for h in range(n_heads):
        j = h // group
        q = q_ref[:, h * HEAD_DIM:(h + 1) * HEAD_DIM]
        s = _dot(q, kt_ref[j * HEAD_DIM:(j + 1) * HEAD_DIM, :])
        if dilated:
            s = s + (log_mult - slopes[h] * distf)
        m = jnp.max(s, axis=-1, keepdims=True)
        p = jnp.exp(s - m)
        l = jnp.sum(p, axis=-1, keepdims=True)
        o = _dot(p.astype(BF16), v_ref[...])[:, j * HEAD_DIM:(j + 1) * HEAD_DIM]
        outs.append(o / l)
    o_ref[...] = jnp.concatenate(outs, axis=1).astype(o_ref.dtype)


def _attention(q, kt, v, seq, *, dilated, n_heads, n_kv):
    t, qw = q.shape
    bsz = t // seq
    tq = TQ_ATTN
    per_seq = seq // tq
    return pl.pallas_call(
        functools.partial(_attn_kernel, dilated=dilated, n_heads=n_heads, n_kv=n_kv),
        grid=(bsz, per_seq),
        in_specs=[
            pl.BlockSpec((tq, qw), lambda b, i: (b * per_seq + i, 0)),
            pl.BlockSpec((kt.shape[0], seq), lambda b, i: (0, b)),
            pl.BlockSpec((seq, v.shape[1]), lambda b, i: (b, 0)),
        ],
        out_specs=pl.BlockSpec((tq, qw), lambda b, i: (b * per_seq + i, 0)),
        out_shape=jax.ShapeDtypeStruct((t, qw), BF16),
        compiler_params=_cparams("arbitrary", "arbitrary"),
        name="attn_dilated" if dilated else "attn_rope",
    )(q, kt, v)


def _hgrn_tri_constants():
    c = HG_CHUNK
    t = np.arange(c)
    mid = HG_BLK * (t // HG_BLK) + HG_MID
    lower = (t[None, :] <= t[:, None])
    upper = (t[None, :] >= t[:, None])
    fwd = np.concatenate([lower, lower[mid]], axis=0)
    bwd = np.concatenate([upper, upper[mid]], axis=0)
    return jnp.asarray(np.stack([fwd, bwd]).astype(np.float32), dtype=BF16)


def _split3_cols(x):
    hi = x.astype(BF16)
    r = x - hi.astype(F32)
    mid = r.astype(BF16)
    lo = (r - mid.astype(F32)).astype(BF16)
    return jnp.concatenate([hi, mid, lo], axis=1)


def _hgrn_direction(q, fpre, v, lb, tri, state_ref, reverse):
    c, w = q.shape
    nblk = c // HG_BLK
    f = jnp.maximum(lb + (1.0 - lb) * jax.nn.sigmoid(fpre), F_MIN)
    lf = jnp.log(f)
    kk = 1.0 - f
    r = _dot(tri, _split3_cols(lf))
    cum = r[:, 2 * w:] + r[:, w:2 * w] + r[:, :w]
    b, bmid = cum[:c], cum[c:]
    blast = b[0:1] if reverse else b[c - 1:c]
    qe = (q * jnp.exp(b)).astype(BF16)
    kdec = (kk * jnp.exp(blast - b)).astype(BF16)
    ktil = kk * jnp.exp(bmid - b)
    row = lax.broadcasted_iota(jnp.int32, (c, c), 0)
    col = lax.broadcasted_iota(jnp.int32, (c, c), 1)
    tri_mask = (col >= row) if reverse else (col <= row)
    brow = lax.broadcasted_iota(jnp.int32, (c, nblk * HEAD_DIM), 0) // HG_BLK
    bcol = lax.broadcasted_iota(jnp.int32, (c, nblk * HEAD_DIM), 1) // HEAD_DIM
    blk_mask = brow == bcol
    vb = v.astype(BF16)
    outs = []
    for h in range(w // HEAD_DIM):
        sl = slice(h * HEAD_DIM, (h + 1) * HEAD_DIM)
        qh, bh = q[:, sl], b[:, sl]
        pieces = []
        for j in range(nblk):
            ref_row = bh[j * HG_BLK + HG_MID:j * HG_BLK + HG_MID + 1, :]
            if reverse:
                hi_row = (j + 1) * HG_BLK
                qt = qh[:hi_row] * jnp.exp(bh[:hi_row] - ref_row)
                if hi_row < c:
                    qt = jnp.concatenate([qt, jnp.zeros((c - hi_row, HEAD_DIM), F32)], axis=0)
            else:
                lo_row = j * HG_BLK
                qt = qh[lo_row:] * jnp.exp(bh[lo_row:] - ref_row)
                if lo_row > 0:
                    qt = jnp.concatenate([jnp.zeros((lo_row, HEAD_DIM), F32), qt], axis=0)
            pieces.append(qt)
        qtil = jnp.concatenate(pieces, axis=1).astype(BF16)
        kt = jnp.concatenate([ktil[:, sl]] * nblk, axis=1)
        kt = jnp.where(blk_mask, kt, 0.0).astype(BF16)
        attn = jnp.where(tri_mask, _dot_nt(qtil, kt), 0.0)
        st = state_ref[h]
        o = _dot(attn.astype(BF16), vb[:, sl]) + _dot_nt(qe[:, sl], st.astype(BF16))
        state_ref[h] = st * jnp.exp(blast[:, sl]) + _dot_tn(vb[:, sl], kdec[:, sl])
        outs.append(o)
    return jnp.concatenate(outs, axis=1)


def _hgrn_kernel(qf_ref, ff_ref, vf_ref, qb_ref, fb_ref, vb_ref, lbf_ref, lbb_ref, tri_ref,
                 of_ref, ob_ref, sf_ref, sb_ref):
    @pl.when(pl.program_id(1) == 0)
    def _():
        sf_ref[...] = jnp.zeros_like(sf_ref)
        sb_ref[...] = jnp.zeros_like(sb_ref)

    of_ref[...] = _hgrn_direction(qf_ref[...], ff_ref[...], vf_ref[...], lbf_ref[...], tri_ref[0],
                                  sf_ref, False)
    ob_ref[...] = _hgrn_direction(qb_ref[...], fb_ref[...], vb_ref[...], lbb_ref[...], tri_ref[1],
                                  sb_ref, True)


def _hgrn(bmix, lbf, lbb, tri, seq):
    t = bmix.shape[0]
    bsz = t // seq
    c = HG_CHUNK
    n = seq // c
    fwd = lambda k: pl.BlockSpec((c, B_W), lambda b, j: (b * n + j, k))
    bwd = lambda k: pl.BlockSpec((c, B_W), lambda b, j: (b * n + n - 1 - j, k))
    const2 = lambda b, j: (0, 0)
    return pl.pallas_call(
        _hgrn_kernel,
        grid=(bsz, n),
        in_specs=[fwd(0), fwd(1), fwd(3), bwd(0), bwd(2), bwd(3),
                  pl.BlockSpec((1, B_W), const2), pl.BlockSpec((1, B_W), const2),
                  pl.BlockSpec(tri.shape, lambda b, j: (0, 0, 0))],
        out_specs=[fwd(0), bwd(0)],
        out_shape=[jax.ShapeDtypeStruct((t, B_W), F32), jax.ShapeDtypeStruct((t, B_W), F32)],
        scratch_shapes=[pltpu.VMEM((B_HEADS, HEAD_DIM, HEAD_DIM), F32),
                        pltpu.VMEM((B_HEADS, HEAD_DIM, HEAD_DIM), F32)],
        compiler_params=_cparams("arbitrary", "arbitrary"),
        name="hgrn",
    )(bmix, bmix, bmix, bmix, bmix, bmix, lbf, lbb, tri)


def _out_proj_kernel(x_ref, oa_ref, of_ref, ob_ref, bg_ref, oc_ref, w_ref, gb_ref, bd_ref,
                     g1_ref, sh_ref, sc_ref, g_ref, xo_ref, h_ref):
    ob = of_ref[...] + ob_ref[...]
    ob = ob * lax.rsqrt(_head_mean_sq(ob, bd_ref[...]) + EPS) * gb_ref[...]
    ob = (ob * _silu(bg_ref[...])).astype(BF16)
    mix = (_dot(oa_ref[...], w_ref[:A_W, :]) + _dot(ob, w_ref[A_W:A_W + B_W, :])
           + _dot(oc_ref[...], w_ref[A_W + B_W:, :]))
    x = x_ref[...] + g1_ref[...] * mix
    xo_ref[...] = x
    y = x * lax.rsqrt(jnp.mean(x * x, axis=-1, keepdims=True) + EPS) * g_ref[...]
    h_ref[...] = (y * (1.0 + sc_ref[...]) + sh_ref[...]).astype(BF16)


def _out_proj(x2, oa, of, ob, bmix, oc, w, gb, bd, mod3, g, seq):
    t, d = x2.shape
    tm = TM_PROJ
    per_seq = seq // tm
    row = lambda i: (i, 0)
    const = lambda i: (0, 0)
    modspec = lambda k: pl.BlockSpec((None, 1, d), lambda i: (i // per_seq, 0, k))
    return pl.pallas_call(
        _out_proj_kernel,
        grid=(t // tm,),
        in_specs=[
            pl.BlockSpec((tm, d), row),
            pl.BlockSpec((tm, A_W), row),
            pl.BlockSpec((tm, B_W), row), pl.BlockSpec((tm, B_W), row),
            pl.BlockSpec((tm, B_W), lambda i: (i, 4)),
            pl.BlockSpec((tm, C_W), row),
            pl.BlockSpec(w.shape, const),
            pl.BlockSpec((1, B_W), const),
            pl.BlockSpec(bd.shape, const),
            modspec(2), modspec(3), modspec(4),
            pl.BlockSpec((1, d), const),
        ],
        out_specs=[pl.BlockSpec((tm, d), row), pl.BlockSpec((tm, d), row)],
        out_shape=[jax.ShapeDtypeStruct((t, d), F32), jax.ShapeDtypeStruct((t, d), BF16)],
        compiler_params=_cparams("arbitrary"),
        name="out_proj",
    )(x2, oa, of, ob, bmix, oc, w, gb, bd, mod3, mod3, mod3, g)


def _ffn_up_kernel(h_ref, wa_ref, wb_ref, cwa_ref, cwb_ref, cba_ref, cbb_ref, o_ref):
    h = h_ref[...]
    seq = h.shape[0]
    row = lax.broadcasted_iota(jnp.int32, (seq, 1), 0)

    def conv(u, cw_ref, cb_ref):
        prev = jnp.where(row == 0, 0.0, pltpu.roll(u, 1, 0))
        nxt = jnp.where(row == seq - 1, 0.0, pltpu.roll(u, seq - 1, 0))
        return cw_ref[0:1, :] * prev + cw_ref[1:2, :] * u + cw_ref[2:3, :] * nxt + cb_ref[...]

    a = conv(_dot(h, wa_ref[...]), cwa_ref, cba_ref)
    b = conv(_dot(h, wb_ref[...]), cwb_ref, cbb_ref)
    o_ref[...] = (_silu(a) * b).astype(BF16)


def _ffn_up(h, w_up, conv_w, conv_b, seq):
    t, d = h.shape
    dff = w_up.shape[1] // 2
    tn = TN_FFN
    nt = dff // tn
    lo = lambda b, j: (0, j)
    hi = lambda b, j: (0, nt + j)
    return pl.pallas_call(
        _ffn_up_kernel,
        grid=(t // seq, nt),
        in_specs=[
            pl.BlockSpec((seq, d), lambda b, j: (b, 0)),
            pl.BlockSpec((d, tn), lo), pl.BlockSpec((d, tn), hi),
            pl.BlockSpec((CONV_W, tn), lo), pl.BlockSpec((CONV_W, tn), hi),
            pl.BlockSpec((1, tn), lo), pl.BlockSpec((1, tn), hi),
        ],
        out_specs=pl.BlockSpec((seq, tn), lambda b, j: (b, j)),
        out_shape=jax.ShapeDtypeStruct((t, dff), BF16),
        compiler_params=_cparams("arbitrary", "arbitrary"),
        name="ffn_up",
    )(h, w_up, w_up, conv_w, conv_w, conv_b, conv_b)


def _ffn_down_kernel(x_ref, a_ref, w_ref, g2_ref, o_ref):
    o_ref[...] = x_ref[...] + g2_ref[...] * _dot(a_ref[...], w_ref[...])


def _ffn_down(x2, act, w, mod3, seq):
    t, d = x2.shape
    tm = TM_PROJ
    per_seq = seq // tm
    row = lambda i: (i, 0)
    return pl.pallas_call(
        _ffn_down_kernel,
        grid=(t // tm,),
        in_specs=[
            pl.BlockSpec((tm, d), row),
            pl.BlockSpec((tm, act.shape[1]), row),
            pl.BlockSpec(w.shape, lambda i: (0, 0)),
            pl.BlockSpec((None, 1, d), lambda i: (i // per_seq, 0, 5)),
        ],
        out_specs=pl.BlockSpec((tm, d), row),
        out_shape=jax.ShapeDtypeStruct((t, d), F32),
        compiler_params=_cparams("arbitrary"),
        name="ffn_down",
    )(x2, act, w, mod3)


def _rope_tables(seq):
    n_rows = seq // GRID_W
    rowp = np.repeat(np.arange(n_rows), GRID_W).astype(np.float32)
    colp = np.tile(np.arange(GRID_W), n_rows).astype(np.float32)
    half = HEAD_DIM // 2
    inv = (np.float32(ROPE_THETA) ** (-np.arange(0, half, 2, dtype=np.float32) / half)).astype(np.float32)
    ang_r = rowp[:, None] * inv
    ang_c = colp[:, None] * inv
    cos = np.concatenate([np.cos(ang_r), np.cos(ang_r), np.cos(ang_c), np.cos(ang_c)], axis=1)
    sin = np.concatenate([-np.sin(ang_r), np.sin(ang_r), -np.sin(ang_c), np.sin(ang_c)], axis=1)
    tile = lambda a: jnp.asarray(np.tile(a.astype(np.float32), (1, A_HEADS)))
    return tile(cos), tile(sin)


def _head_block_ones(width):
    i = np.arange(width) // HEAD_DIM
    return jnp.asarray((i[:, None] == i[None, :]).astype(np.float32), dtype=BF16)


def kernel(x, c, w_ada, b_ada, norm_g, w_in, a_q_norm, a_k_norm, b_lb, b_out_norm, c_q_norm, c_k_norm,
           w_out, w_up, conv_w, conv_b, w_down):
    bsz, seq, d = x.shape
    depth = w_in.shape[0]
    t = bsz * seq
    scale = HEAD_DIM ** -0.5

    mod = _ada(c, w_ada, b_ada)
    lb_all = _hgrn_lb(b_lb.astype(F32))
    cos, sin = _rope_tables(seq)
    bd = _head_block_ones(A_W)
    tri = _hgrn_tri_constants()
    tile = lambda gvec, heads, s=1.0: (jnp.tile(gvec.astype(F32), heads) * s).reshape(1, -1)

    x2 = x.reshape(t, d)
    for l in range(depth):
        mod3 = mod[l].reshape(bsz, 1, 6 * d)
        aq, akt, av, bmix, cq, ckt, cv = _in_proj(
            x2, mod3, norm_g[l, 0].reshape(1, d), w_in[l].astype(BF16), cos, sin,
            tile(a_q_norm[l], A_HEADS, scale), tile(a_k_norm[l], A_KV_HEADS),
            tile(c_q_norm[l], C_HEADS, scale), tile(c_k_norm[l], C_KV_HEADS), bd, seq)
        o_a = _attention(aq, akt, av, seq, dilated=False, n_heads=A_HEADS, n_kv=A_KV_HEADS)
        o_c = _attention(cq, ckt, cv, seq, dilated=True, n_heads=C_HEADS, n_kv=C_KV_HEADS)
        o_f, o_b = _hgrn(bmix, lb_all[0, l].reshape(1, B_W), lb_all[1, l].reshape(1, B_W), tri, seq)
        x2, h2 = _out_proj(x2, o_a, o_f, o_b, bmix, o_c, w_out[l].astype(BF16),
                           tile(b_out_norm[l], B_HEADS), bd[:B_W, :B_W], mod3,
                           norm_g[l, 1].reshape(1, d), seq)
        act = _ffn_up(h2, w_up[l].astype(BF16), conv_w[l], conv_b[l].reshape(1, -1), seq)
        x2 = _ffn_down(x2, act, w_down[l].astype(BF16), mod3, seq)
    return x2.reshape(bsz, seq, d)
```

```python
import functools

import numpy as np
import jax
import jax.numpy as jnp
from jax import lax
from jax.experimental import pallas as pl
from jax.experimental.pallas import tpu as pltpu

F32 = jnp.float32
BF16 = jnp.bfloat16

HEAD_DIM = 64
GRID_W = 64
EPS = 1e-6
NEG_BIG = -1e30
F_MIN = 1e-6
ROPE_THETA = 10000.0
A_HEADS, A_KV_HEADS = 6, 2
B_HEADS = 4
C_HEADS, C_KV_HEADS = 6, 2
C_BRANCHES = ((128, 1), (512, 4), (2048, 16))
CONV_W = 3

A_W = A_HEADS * HEAD_DIM
KV_W = A_KV_HEADS * HEAD_DIM
B_W = B_HEADS * HEAD_DIM
C_W = C_HEADS * HEAD_DIM
OFF_AQ, OFF_AK, OFF_AV = 0, A_W, A_W + KV_W
OFF_B = A_W + 2 * KV_W
OFF_CQ = OFF_B + 5 * B_W
OFF_CK, OFF_CV = OFF_CQ + C_W, OFF_CQ + C_W + KV_W

TM_PROJ = 512
TQ_ATTN = 256
ATTN_AHEAD = 2
HG_CHUNK = 64
HG_BLK = 8
HG_MID = 4
TN_FFN = 256
RC_PROJ = 128
RC_FFN = 128
TN_ADA = 1536
VMEM_LIMIT = 56 * 1024 * 1024


def _cparams(*sem):
    return pltpu.CompilerParams(dimension_semantics=sem, vmem_limit_bytes=VMEM_LIMIT)


def _dot(a, b):
    return jnp.dot(a, b, preferred_element_type=F32)


def _dot_nt(a, b):
    return lax.dot_general(a, b, (((1,), (1,)), ((), ())), preferred_element_type=F32)


def _dot_tn(a, b):
    return lax.dot_general(a, b, (((0,), (0,)), ((), ())), preferred_element_type=F32)


def _split2(x):
    hi = x.astype(BF16)
    lo = (x - hi.astype(F32)).astype(BF16)
    return hi, lo


def _head_mean_sq(x, bd):
    hi, lo = _split2(x * x)
    return (_dot(hi, bd) + _dot(lo, bd)) * (1.0 / HEAD_DIM)


def _silu(x):
    return x * jax.nn.sigmoid(x)


def _ada_kernel(c_ref, w_ref, b_ref, o_ref):
    a = _silu(c_ref[...]).astype(BF16)
    o_ref[...] = _dot(a, w_ref[...].astype(BF16)) + b_ref[...]


def _ada(c, w_ada, b_ada):
    depth, d, n = w_ada.shape
    bsz = c.shape[0]
    return pl.pallas_call(
        _ada_kernel,
        grid=(depth, n // TN_ADA),
        in_specs=[
            pl.BlockSpec((bsz, d), lambda l, j: (0, 0)),
            pl.BlockSpec((None, d, TN_ADA), lambda l, j: (l, 0, j)),
            pl.BlockSpec((None, 1, TN_ADA), lambda l, j: (l, 0, j)),
        ],
        out_specs=pl.BlockSpec((None, bsz, TN_ADA), lambda l, j: (l, 0, j)),
        out_shape=jax.ShapeDtypeStruct((depth, bsz, n), F32),
        compiler_params=_cparams("arbitrary", "arbitrary"),
        name="ada",
    )(c, w_ada, b_ada.reshape(depth, 1, n))


def _lb_kernel(b_ref, o_ref):
    depth = b_ref.shape[1]
    for d in range(2):
        rows = [b_ref[d, l:l + 1, :] for l in range(depth)]
        m = functools.reduce(jnp.maximum, rows)
        e = [jnp.exp(r - m) for r in rows]
        tot = functools.reduce(lambda a, b: a + b, e)
        run = jnp.zeros_like(m)
        for l in range(depth):
            sm = e[l] / tot
            run = run + sm
            o_ref[d, l:l + 1, :] = run - e[0] / tot


def _hgrn_lb(b_lb):
    return pl.pallas_call(
        _lb_kernel,
        out_shape=jax.ShapeDtypeStruct(b_lb.shape, F32),
        name="hgrn_lb",
    )(b_lb)


def _rope(z, cos, sin):
    n = z.shape[-1]
    lane = lax.broadcasted_iota(jnp.int32, z.shape, 1)
    up = pltpu.roll(z, n - 16, 1)
    dn = pltpu.roll(z, 16, 1)
    return z * cos + jnp.where((lane % 32) < 16, up, dn) * sin


def _in_proj_kernel(x_ref, sh_ref, sc_ref, g_ref, w_ref, cos_ref, sin_ref,
                    gaq_ref, gak_ref, gcq_ref, gck_ref, bd_ref,
                    aq_ref, akt_ref, av_ref, bmix_ref, cq_ref, ck_ref, cv_ref):
    def headnorm(z, gain):
        n = z.shape[-1]
        return z * lax.rsqrt(_head_mean_sq(z, bd_ref[:n, :n]) + EPS) * gain

    x = x_ref[...]
    y = x * lax.rsqrt(jnp.mean(x * x, axis=-1, keepdims=True) + EPS) * g_ref[...]
    h = (y * (1.0 + sc_ref[...]) + sh_ref[...]).astype(BF16)

    def proj(off, width):
        return _dot(h, w_ref[:, off:off + width])

    cos, sin = cos_ref[...], sin_ref[...]
    aq_ref[...] = _rope(headnorm(proj(OFF_AQ, A_W), gaq_ref[...]), cos, sin).astype(BF16)
    ak = _rope(headnorm(proj(OFF_AK, KV_W), gak_ref[...]), cos[:, :KV_W], sin[:, :KV_W])
    akt_ref[...] = ak.T.astype(BF16)
    av_ref[...] = proj(OFF_AV, KV_W).astype(BF16)
    bmix_ref[...] = proj(OFF_B, 5 * B_W)
    cq_ref[...] = headnorm(proj(OFF_CQ, C_W), gcq_ref[...])
    ck_ref[...] = headnorm(proj(OFF_CK, KV_W), gck_ref[...])
    cv_ref[...] = proj(OFF_CV, KV_W)


def _in_proj(x2, mod3, g, w, cos, sin, gaq, gak, gcq, gck, bd, seq):
    t, d = x2.shape
    tm = TM_PROJ
    per_seq = seq // tm
    row = lambda i: (i, 0)
    const = lambda i: (0, 0)
    modspec = lambda k: pl.BlockSpec((None, 1, d), lambda i: (i // per_seq, 0, k))
    return pl.pallas_call(
        _in_proj_kernel,
        grid=(t // tm,),
        in_specs=[
            pl.BlockSpec((tm, d), row),
            modspec(0), modspec(1),
            pl.BlockSpec((1, d), const),
            pl.BlockSpec(w.shape, const),
            pl.BlockSpec((tm, A_W), lambda i: (i % per_seq, 0)),
            pl.BlockSpec((tm, A_W), lambda i: (i % per_seq, 0)),
            pl.BlockSpec((1, A_W), const), pl.BlockSpec((1, KV_W), const),
            pl.BlockSpec((1, C_W), const), pl.BlockSpec((1, KV_W), const),
            pl.BlockSpec(bd.shape, const),
        ],
        out_specs=[
            pl.BlockSpec((tm, A_W), row),
            pl.BlockSpec((KV_W, tm), lambda i: (0, i)),
            pl.BlockSpec((tm, KV_W), row),
            pl.BlockSpec((tm, 5 * B_W), row),
            pl.BlockSpec((tm, C_W), row),
            pl.BlockSpec((tm, KV_W), row),
            pl.BlockSpec((tm, KV_W), row),
        ],
        out_shape=[
            jax.ShapeDtypeStruct((t, A_W), BF16),
            jax.ShapeDtypeStruct((KV_W, t), BF16),
            jax.ShapeDtypeStruct((t, KV_W), BF16),
            jax.ShapeDtypeStruct((t, 5 * B_W), F32),
            jax.ShapeDtypeStruct((t, C_W), F32),
            jax.ShapeDtypeStruct((t, KV_W), F32),
            jax.ShapeDtypeStruct((t, KV_W), F32),
        ],
        compiler_params=_cparams("arbitrary"),
        name="in_proj",
    )(x2, mod3, mod3, g, w, cos, sin, gaq, gak, gcq, gck, bd)


def _alibi_slopes():
    return [float(s) for s in (2.0 ** (-8.0 * np.arange(1, C_HEADS + 1) / C_HEADS)).astype(np.float32)]


def _run_ahead(items, start, finish, ahead=ATTN_AHEAD):
    pending = {}
    for k in range(len(items) + ahead):
        if k < len(items):
            pending[k] = start(items[k])
        if k >= ahead:
            finish(items[k - ahead], pending.pop(k - ahead))


def _attn_kernel(q_ref, kt_ref, v_ref, o_ref, *, n_heads, n_kv):
    group = n_heads // n_kv
    outs = []

    def scores(h):
        j = h // group
        return _dot(q_ref[:, h * HEAD_DIM:(h + 1) * HEAD_DIM], kt_ref[j * HEAD_DIM:(j + 1) * HEAD_DIM, :])

    def finish(h, s):
        j = h // group
        m = jnp.max(s, axis=-1, keepdims=True)
        p = jnp.exp(s - m)
        l = jnp.sum(p, axis=-1, keepdims=True)
        o = _dot(p.astype(BF16), v_ref[...])[:, j * HEAD_DIM:(j + 1) * HEAD_DIM]
        outs.append(o / l)

    _run_ahead(list(range(n_heads)), scores, finish)
    o_ref[...] = jnp.concatenate(outs, axis=1).astype(o_ref.dtype)


def _attention(q, kt, v, seq, *, n_heads, n_kv):
    t, qw = q.shape
    bsz = t // seq
    tq = TQ_ATTN
    per_seq = seq // tq
    return pl.pallas_call(
        functools.partial(_attn_kernel, n_heads=n_heads, n_kv=n_kv),
        grid=(bsz, per_seq),
        in_specs=[
            pl.BlockSpec((tq, qw), lambda b, i: (b * per_seq + i, 0)),
            pl.BlockSpec((kt.shape[0], seq), lambda b, i: (0, b)),
            pl.BlockSpec((seq, v.shape[1]), lambda b, i: (b, 0)),
        ],
        out_specs=pl.BlockSpec((tq, qw), lambda b, i: (b * per_seq + i, 0)),
        out_shape=jax.ShapeDtypeStruct((t, qw), BF16),
        compiler_params=_cparams("arbitrary", "arbitrary"),
        name="attn_rope",
    )(q, kt, v)


DIL_RES = 4
DIL_TQ = 128
DIL_WIN = 3 * DIL_TQ
PAIR_W = 2 * HEAD_DIM
assert C_BRANCHES == ((128, 1), (512, 4), (2048, 16)) and C_BRANCHES[0][0] // 2 <= DIL_TQ


def _dilated_bias_tables(seq):
    slopes = jnp.asarray(_alibi_slopes(), F32).reshape(C_HEADS, 1, 1)
    n = seq // DIL_RES
    i = jnp.arange(n, dtype=jnp.int32)
    a = jnp.abs(i[None, :] - i[:, None])
    w2, w3 = C_BRANCHES[1][0] // (2 * DIL_RES), C_BRANCHES[2][0] // (2 * DIL_RES)
    step3 = C_BRANCHES[2][1] // DIL_RES
    count = (a <= w2).astype(jnp.int32) + ((a <= w3) & (a % step3 == 0)).astype(jnp.int32)
    log_mult = jnp.where(count == 2, float(np.log(2.0)), jnp.where(count == 1, 0.0, NEG_BIG))
    strided = log_mult[None] - slopes * (DIL_RES * a).astype(F32)[None]
    r = jnp.arange(DIL_TQ, dtype=jnp.int32)[:, None]
    c = jnp.arange(DIL_WIN, dtype=jnp.int32)[None, :]
    half = C_BRANCHES[0][0] // 2
    local = []
    for shift in (0, DIL_TQ, 2 * DIL_TQ):
        d = jnp.abs(c - shift - r)
        local.append(jnp.where((d <= half)[None], -slopes * d.astype(F32)[None], NEG_BIG))
    return strided, jnp.stack(local)


def _kv_variants(kt, v_f32):
    zeros = jnp.zeros((HEAD_DIM, kt.shape[1]), BF16)
    ktz = [[jnp.concatenate([kt[j * HEAD_DIM:(j + 1) * HEAD_DIM], zeros], axis=0) for j in range(2)],
           [jnp.concatenate([zeros, kt[j * HEAD_DIM:(j + 1) * HEAD_DIM]], axis=0) for j in range(2)]]
    v = v_f32.astype(BF16)
    vswap = pltpu.roll(v_f32, HEAD_DIM, 1).astype(BF16)
    vsel = [[v, vswap], [vswap, v]]
    return ktz, vsel


def _dil_strided_kernel(q0_ref, q1_ref, q2_ref, k_ref, v_ref, bias_ref,
                        o0_ref, o1_ref, o2_ref, l0_ref, l1_ref, l2_ref):
    n = k_ref.shape[0] // DIL_RES
    rows = pl.ds(pl.program_id(1), n, stride=DIL_RES)
    group = C_HEADS // C_KV_HEADS
    ktz, vsel = _kv_variants(k_ref[rows, :].T.astype(BF16), v_ref[rows, :])
    left = lax.broadcasted_iota(jnp.int32, (n, PAIR_W), 1) < HEAD_DIM
    slabs = ((q0_ref, o0_ref, l0_ref), (q1_ref, o1_ref, l1_ref), (q2_ref, o2_ref, l2_ref))
    qs = [q_ref[rows, :].astype(BF16) for q_ref, _, _ in slabs]
    res, lse = [], []

    def scores(h):
        return _dot(qs[h // 2], ktz[h % 2][h // group]) + bias_ref[h]

    def finish(h, s):
        m = jnp.max(s, axis=-1, keepdims=True)
        p = jnp.exp(s - m)
        l = jnp.sum(p, axis=-1, keepdims=True)
        res.append(_dot(p.astype(BF16), vsel[h % 2][h // group]) / l)
        lse.append(m + jnp.log(l))
        if h % 2 == 1:
            _, o_ref, l_ref = slabs[h // 2]
            o_ref[rows, :] = jnp.where(left, res[-2], res[-1])
            l_ref[rows, :] = jnp.where(left, lse[-2], lse[-1])

    _run_ahead(list(range(C_HEADS)), scores, finish)


def _dil_strided(cq, ck, cv, bias, seq):
    t = cq.shape[0]
    bsz = t // seq
    blk = lambda col: pl.BlockSpec((seq, PAIR_W), lambda b, r: (b, col))
    slab = jax.ShapeDtypeStruct((t, PAIR_W), F32)
    return pl.pallas_call(
        _dil_strided_kernel,
        grid=(bsz, DIL_RES),
        in_specs=[blk(0), blk(1), blk(2), blk(0), blk(0),
                  pl.BlockSpec(bias.shape, lambda b, r: (0, 0, 0))],
        out_specs=[blk(0)] * 6,
        out_shape=[slab] * 6,
        compiler_params=_cparams("arbitrary", "arbitrary"),
        name="attn_dil_strided",
    )(cq, cq, cq, ck, cv, bias)


def _dil_local_kernel(q0_ref, q1_ref, q2_ref, k_ref, v_ref, bias_ref,
                      f0_ref, f1_ref, f2_ref, l0_ref, l1_ref, l2_ref, o_ref):
    seq = k_ref.shape[0]
    tq, win = DIL_TQ, DIL_WIN
    group = C_HEADS // C_KV_HEADS
    ktz, vsel = _kv_variants(k_ref[...].T.astype(BF16), v_ref[...])
    left = lax.broadcasted_iota(jnp.int32, (tq, PAIR_W), 1) < HEAD_DIM
    slabs = ((q0_ref, f0_ref, l0_ref), (q1_ref, f1_ref, l1_ref), (q2_ref, f2_ref, l2_ref))
    items = [(i, slab, half) for i in range(seq // tq) for slab in range(3) for half in range(2)]

    def window(i):
        t0 = i * tq
        w0 = min(max(t0 - tq, 0), seq - win)
        return t0, w0, (t0 - w0) // tq

    def scores(item):
        i, slab, half = item
        t0, w0, variant = window(i)
        h = 2 * slab + half
        q = slabs[slab][0][t0:t0 + tq, :].astype(BF16)
        return _dot(q, ktz[half][h // group][:, w0:w0 + win]) + bias_ref[variant, h]

    def finish(item, s, acc):
        i, slab, half = item
        t0, w0, _ = window(i)
        m = jnp.max(s, axis=-1, keepdims=True)
        p = jnp.exp(s - m)
        near = _dot(p.astype(BF16), vsel[half][(2 * slab + half) // group][w0:w0 + win, :])
        acc.append((near, m, jnp.sum(p, axis=-1, keepdims=True)))
        if half == 0:
            return
        (n0, m0, s0), (n1, m1, s1) = acc[-2], acc[-1]
        near, m_near, l_near = jnp.where(left, n0, n1), jnp.where(left, m0, m1), jnp.where(left, s0, s1)
        _, far_ref, lse_ref = slabs[slab]
        lse_far = lse_ref[t0:t0 + tq, :]
        top = jnp.maximum(m_near, lse_far)
        w_near = jnp.exp(m_near - top)
        w_far = jnp.exp(lse_far - top)
        merged = (near * w_near + far_ref[t0:t0 + tq, :] * w_far) / (l_near * w_near + w_far)
        o_ref[t0:t0 + tq, slab * PAIR_W:(slab + 1) * PAIR_W] = merged.astype(o_ref.dtype)

    acc = []
    _run_ahead(items, scores, lambda item, s: finish(item, s, acc))


def _dil_local(cq, ck, cv, bias, far, seq):
    t = cq.shape[0]
    blk = lambda col: pl.BlockSpec((seq, PAIR_W), lambda b: (b, col))
    return pl.pallas_call(
        _dil_local_kernel,
        grid=(t // seq,),
        in_specs=[blk(0), blk(1), blk(2), blk(0), blk(0),
                  pl.BlockSpec(bias.shape, lambda b: (0, 0, 0, 0))] + [blk(0)] * 6,
        out_specs=pl.BlockSpec((seq, C_W), lambda b: (b, 0)),
        out_shape=jax.ShapeDtypeStruct((t, C_W), BF16),
        compiler_params=_cparams("arbitrary"),
        name="attn_dil_local",
    )(cq, cq, cq, ck, cv, bias, *far)


def _hgrn_tri_constants():
    c = HG_CHUNK
    t = np.arange(c)
    mid = HG_BLK * (t // HG_BLK) + HG_MID
    lower = (t[None, :] <= t[:, None])
    upper = (t[None, :] >= t[:, None])
    fwd = np.concatenate([lower, lower[mid]], axis=0)
    bwd = np.concatenate([upper, upper[mid]], axis=0)
    return jnp.asarray(np.stack([fwd, bwd]).astype(np.float32), dtype=BF16)


def _split3_cols(x):
    hi = x.astype(BF16)
    r = x - hi.astype(F32)
    mid = r.astype(BF16)
    lo = (r - mid.astype(F32)).astype(BF16)
    return jnp.concatenate([hi, mid, lo], axis=1)


def _hgrn_prefix(q, fpre, v, lb, tri, reverse):
    c, w = q.shape
    f = jnp.maximum(lb + (1.0 - lb) * jax.nn.sigmoid(fpre), F_MIN)
    lf = jnp.log(f)
    kk = 1.0 - f
    r = _dot(tri, _split3_cols(lf))
    cum = r[:, 2 * w:] + r[:, w:2 * w] + r[:, :w]
    b, bmid = cum[:c], cum[c:]
    blast = b[0:1] if reverse else b[c - 1:c]
    return dict(q=q, b=b, blast=blast, vb=v.astype(BF16), reverse=reverse,
                qe=(q * jnp.exp(b)).astype(BF16),
                kdec=(kk * jnp.exp(blast - b)).astype(BF16),
                ktil=kk * jnp.exp(bmid - b))


def _hgrn_tilde(pre, sl, blk_mask):
    qh, bh, reverse = pre["q"][:, sl], pre["b"][:, sl], pre["reverse"]
    c = qh.shape[0]
    nblk = c // HG_BLK
    pieces = []
    for j in range(nblk):
        ref_row = bh[j * HG_BLK + HG_MID:j * HG_BLK + HG_MID + 1, :]
        if reverse:
            hi_row = (j + 1) * HG_BLK
            qt = qh[:hi_row] * jnp.exp(bh[:hi_row] - ref_row)
            if hi_row < c:
                qt = jnp.concatenate([qt, jnp.zeros((c - hi_row, HEAD_DIM), F32)], axis=0)
        else:
            lo_row = j * HG_BLK
            qt = qh[lo_row:] * jnp.exp(bh[lo_row:] - ref_row)
            if lo_row > 0:
                qt = jnp.concatenate([jnp.zeros((lo_row, HEAD_DIM), F32), qt], axis=0)
        pieces.append(qt)
    qtil = jnp.concatenate(pieces, axis=1).astype(BF16)
    kt = jnp.concatenate([pre["ktil"][:, sl]] * nblk, axis=1)
    return qtil, jnp.where(blk_mask, kt, 0.0).astype(BF16)


def _hgrn_kernel(qf_ref, ff_ref, vf_ref, qb_ref, fb_ref, vb_ref, lbf_ref, lbb_ref, tri_ref,
                 of_ref, ob_ref, sf_ref, sb_ref):
    @pl.when(pl.program_id(1) == 0)
    def _():
        sf_ref[...] = jnp.zeros_like(sf_ref)
        sb_ref[...] = jnp.zeros_like(sb_ref)

    c = qf_ref.shape[0]
    nblk = c // HG_BLK
    row = lax.broadcasted_iota(jnp.int32, (c, c), 0)
    col = lax.broadcasted_iota(jnp.int32, (c, c), 1)
    brow = lax.broadcasted_iota(jnp.int32, (c, nblk * HEAD_DIM), 0) // HG_BLK
    bcol = lax.broadcasted_iota(jnp.int32, (c, nblk * HEAD_DIM), 1) // HEAD_DIM
    blk_mask = brow == bcol

    pres = [_hgrn_prefix(qf_ref[...], ff_ref[...], vf_ref[...], lbf_ref[...], tri_ref[0], False),
            _hgrn_prefix(qb_ref[...], fb_ref[...], vb_ref[...], lbb_ref[...], tri_ref[1], True)]
    units = [(d, h) for d in range(2) for h in range(B_HEADS)]
    sl = lambda h: slice(h * HEAD_DIM, (h + 1) * HEAD_DIM)
    state_refs = (sf_ref, sb_ref)

    inter = {}
    for d, h in units:
        pre, st = pres[d], state_refs[d][h]
        inter[d, h] = _dot_nt(pre["qe"][:, sl(h)], st.astype(BF16))
        state_refs[d][h] = (st * jnp.exp(pre["blast"][:, sl(h)])
                            + _dot_tn(pre["vb"][:, sl(h)], pre["kdec"][:, sl(h)]))
    attn = {}
    for d, h in units:
        qtil, kt = _hgrn_tilde(pres[d], sl(h), blk_mask)
        attn[d, h] = _dot_nt(qtil, kt)
    outs = ([], [])
    for d, h in units:
        keep = (col >= row) if pres[d]["reverse"] else (col <= row)
        a = jnp.where(keep, attn[d, h], 0.0).astype(BF16)
        outs[d].append(_dot(a, pres[d]["vb"][:, sl(h)]) + inter[d, h])
    of_ref[...] = jnp.concatenate(outs[0], axis=1)
    ob_ref[...] = jnp.concatenate(outs[1], axis=1)


def _hgrn(bmix, lbf, lbb, tri, seq):
    t = bmix.shape[0]
    bsz = t // seq
    c = HG_CHUNK
    n = seq // c
    fwd = lambda k: pl.BlockSpec((c, B_W), lambda b, j: (b * n + j, k))
    bwd = lambda k: pl.BlockSpec((c, B_W), lambda b, j: (b * n + n - 1 - j, k))
    const2 = lambda b, j: (0, 0)
    return pl.pallas_call(
        _hgrn_kernel,
        grid=(bsz, n),
        in_specs=[fwd(0), fwd(1), fwd(3), bwd(0), bwd(2), bwd(3),
                  pl.BlockSpec((1, B_W), const2), pl.BlockSpec((1, B_W), const2),
                  pl.BlockSpec(tri.shape, lambda b, j: (0, 0, 0))],
        out_specs=[fwd(0), bwd(0)],
        out_shape=[jax.ShapeDtypeStruct((t, B_W), F32), jax.ShapeDtypeStruct((t, B_W), F32)],
        scratch_shapes=[pltpu.VMEM((B_HEADS, HEAD_DIM, HEAD_DIM), F32),
                        pltpu.VMEM((B_HEADS, HEAD_DIM, HEAD_DIM), F32)],
        compiler_params=_cparams("arbitrary", "arbitrary"),
        name="hgrn",
    )(bmix, bmix, bmix, bmix, bmix, bmix, lbf, lbb, tri)


def _out_proj_kernel(x_ref, oa_ref, of_ref, ob_ref, bg_ref, oc_ref, w_ref, gb_ref, bd_ref,
                     g1_ref, sh_ref, sc_ref, g_ref, xo_ref, h_ref):
    ob = of_ref[...] + ob_ref[...]
    ob = ob * lax.rsqrt(_head_mean_sq(ob, bd_ref[...]) + EPS) * gb_ref[...]
    ob = (ob * _silu(bg_ref[...])).astype(BF16)
    mix = (_dot(oa_ref[...], w_ref[:A_W, :]) + _dot(ob, w_ref[A_W:A_W + B_W, :])
           + _dot(oc_ref[...], w_ref[A_W + B_W:, :]))
    x = x_ref[...] + g1_ref[...] * mix
    xo_ref[...] = x
    y = x * lax.rsqrt(jnp.mean(x * x, axis=-1, keepdims=True) + EPS) * g_ref[...]
    h_ref[...] = (y * (1.0 + sc_ref[...]) + sh_ref[...]).astype(BF16)


def _out_proj(x2, oa, of, ob, bmix, oc, w, gb, bd, mod3, g, seq):
    t, d = x2.shape
    tm = TM_PROJ
    per_seq = seq // tm
    row = lambda i: (i, 0)
    const = lambda i: (0, 0)
    modspec = lambda k: pl.BlockSpec((None, 1, d), lambda i: (i // per_seq, 0, k))
    return pl.pallas_call(
        _out_proj_kernel,
        grid=(t // tm,),
        in_specs=[
            pl.BlockSpec((tm, d), row),
            pl.BlockSpec((tm, A_W), row),
            pl.BlockSpec((tm, B_W), row), pl.BlockSpec((tm, B_W), row),
            pl.BlockSpec((tm, B_W), lambda i: (i, 4)),
            pl.BlockSpec((tm, C_W), row),
            pl.BlockSpec(w.shape, const),
            pl.BlockSpec((1, B_W), const),
            pl.BlockSpec(bd.shape, const),
            modspec(2), modspec(3), modspec(4),
            pl.BlockSpec((1, d), const),
        ],
        out_specs=[pl.BlockSpec((tm, d), row), pl.BlockSpec((tm, d), row)],
        out_shape=[jax.ShapeDtypeStruct((t, d), F32), jax.ShapeDtypeStruct((t, d), BF16)],
        compiler_params=_cparams("arbitrary"),
        name="out_proj",
    )(x2, oa, of, ob, bmix, oc, w, gb, bd, mod3, mod3, mod3, g)


def _ffn_up_kernel(h_ref, wa_ref, wb_ref, cwa_ref, cwb_ref, cba_ref, cbb_ref, o_ref):
    rc = RC_FFN
    n = h_ref.shape[0] // rc
    row = lax.broadcasted_iota(jnp.int32, (rc, 1), 0)

    def dots(c):
        h = h_ref[pl.ds(c * rc, rc), :]
        return _dot(h, wa_ref[...]), _dot(h, wb_ref[...])

    def conv(us, c, cw_ref, cb_ref):
        u = us[c]
        before = us[c - 1][rc - 1:rc, :] if c > 0 else 0.0
        after = us[c + 1][0:1, :] if c + 1 < n else 0.0
        prev = jnp.where(row == 0, before, pltpu.roll(u, 1, 0))
        nxt = jnp.where(row == rc - 1, after, pltpu.roll(u, rc - 1, 0))
        return cw_ref[0:1, :] * prev + cw_ref[1:2, :] * u + cw_ref[2:3, :] * nxt + cb_ref[...]

    ua, ub = [None] * n, [None] * n
    ua[0], ub[0] = dots(0)
    for c in range(n):
        if c + 1 < n:
            ua[c + 1], ub[c + 1] = dots(c + 1)
        a = conv(ua, c, cwa_ref, cba_ref)
        b = conv(ub, c, cwb_ref, cbb_ref)
        o_ref[pl.ds(c * rc, rc), :] = (_silu(a) * b).astype(BF16)


def _ffn_up(h, w_up, conv_w, conv_b, seq):
    t, d = h.shape
    dff = w_up.shape[1] // 2
    tn = TN_FFN
    nt = dff // tn
    lo = lambda b, j: (0, j)
    hi = lambda b, j: (0, nt + j)
    return pl.pallas_call(
        _ffn_up_kernel,
        grid=(t // seq, nt),
        in_specs=[
            pl.BlockSpec((seq, d), lambda b, j: (b, 0)),
            pl.BlockSpec((d, tn), lo), pl.BlockSpec((d, tn), hi),
            pl.BlockSpec((CONV_W, tn), lo), pl.BlockSpec((CONV_W, tn), hi),
            pl.BlockSpec((1, tn), lo), pl.BlockSpec((1, tn), hi),
        ],
        out_specs=pl.BlockSpec((seq, tn), lambda b, j: (b, j)),
        out_shape=jax.ShapeDtypeStruct((t, dff), BF16),
        compiler_params=_cparams("arbitrary", "arbitrary"),
        name="ffn_up",
    )(h, w_up, w_up, conv_w, conv_w, conv_b, conv_b)


def _ffn_down_kernel(x_ref, a_ref, w_ref, g2_ref, o_ref):
    o_ref[...] = x_ref[...] + g2_ref[...] * _dot(a_ref[...], w_ref[...])


def _ffn_down(x2, act, w, mod3, seq):
    t, d = x2.shape
    tm = TM_PROJ
    per_seq = seq // tm
    row = lambda i: (i, 0)
    return pl.pallas_call(
        _ffn_down_kernel,
        grid=(t // tm,),
        in_specs=[
            pl.BlockSpec((tm, d), row),
            pl.BlockSpec((tm, act.shape[1]), row),
            pl.BlockSpec(w.shape, lambda i: (0, 0)),
            pl.BlockSpec((None, 1, d), lambda i: (i // per_seq, 0, 5)),
        ],
        out_specs=pl.BlockSpec((tm, d), row),
        out_shape=jax.ShapeDtypeStruct((t, d), F32),
        compiler_params=_cparams("arbitrary"),
        name="ffn_down",
    )(x2, act, w, mod3)


def _rope_tables(seq):
    n_rows = seq // GRID_W
    rowp = np.repeat(np.arange(n_rows), GRID_W).astype(np.float32)
    colp = np.tile(np.arange(GRID_W), n_rows).astype(np.float32)
    half = HEAD_DIM // 2
    inv = (np.float32(ROPE_THETA) ** (-np.arange(0, half, 2, dtype=np.float32) / half)).astype(np.float32)
    ang_r = rowp[:, None] * inv
    ang_c = colp[:, None] * inv
    cos = np.concatenate([np.cos(ang_r), np.cos(ang_r), np.cos(ang_c), np.cos(ang_c)], axis=1)
    sin = np.concatenate([-np.sin(ang_r), np.sin(ang_r), -np.sin(ang_c), np.sin(ang_c)], axis=1)
    tile = lambda a: jnp.asarray(np.tile(a.astype(np.float32), (1, A_HEADS)))
    return tile(cos), tile(sin)


def _head_block_ones(width):
    i = np.arange(width) // HEAD_DIM
    return jnp.asarray((i[:, None] == i[None, :]).astype(np.float32), dtype=BF16)


def kernel(x, c, w_ada, b_ada, norm_g, w_in, a_q_norm, a_k_norm, b_lb, b_out_norm, c_q_norm, c_k_norm,
           w_out, w_up, conv_w, conv_b, w_down):
    bsz, seq, d = x.shape
    depth = w_in.shape[0]
    t = bsz * seq
    scale = HEAD_DIM ** -0.5

    mod = _ada(c, w_ada, b_ada)
    lb_all = _hgrn_lb(b_lb.astype(F32))
    cos, sin = _rope_tables(seq)
    bd = _head_block_ones(A_W)
    tri = _hgrn_tri_constants()
    bias_strided, bias_local = _dilated_bias_tables(seq)
    tile = lambda gvec, heads, s=1.0: (jnp.tile(gvec.astype(F32), heads) * s).reshape(1, -1)

    x2 = x.reshape(t, d)
    for l in range(depth):
        mod3 = mod[l].reshape(bsz, 1, 6 * d)
        aq, akt, av, bmix, cq, ck, cv = _in_proj(
            x2, mod3, norm_g[l, 0].reshape(1, d), w_in[l].astype(BF16), cos, sin,
            tile(a_q_norm[l], A_HEADS, scale), tile(a_k_norm[l], A_KV_HEADS),
            tile(c_q_norm[l], C_HEADS, scale), tile(c_k_norm[l], C_KV_HEADS), bd, seq)
        o_a = _attention(aq, akt, av, seq, n_heads=A_HEADS, n_kv=A_KV_HEADS)
        far = _dil_strided(cq, ck, cv, bias_strided, seq)
        o_c = _dil_local(cq, ck, cv, bias_local, far, seq)
        o_f, o_b = _hgrn(bmix, lb_all[0, l].reshape(1, B_W), lb_all[1, l].reshape(1, B_W), tri, seq)
        x2, h2 = _out_proj(x2, o_a, o_f, o_b, bmix, o_c, w_out[l].astype(BF16),
                           tile(b_out_norm[l], B_HEADS), bd[:B_W, :B_W], mod3,
                           norm_g[l, 1].reshape(1, d), seq)
        act = _ffn_up(h2, w_up[l].astype(BF16), conv_w[l], conv_b[l].reshape(1, -1), seq)
        x2 = _ffn_down(x2, act, w_down[l].astype(BF16), mod3, seq)
    return x2.reshape(bsz, seq, d)
```

```python
import functools

import numpy as np
import jax
import jax.numpy as jnp
from jax import lax
from jax.experimental import pallas as pl
from jax.experimental.pallas import tpu as pltpu

F32 = jnp.float32
BF16 = jnp.bfloat16

HEAD_DIM = 64
GRID_W = 64
EPS = 1e-6
NEG_BIG = -1e30
F_MIN = 1e-6
ROPE_THETA = 10000.0
A_HEADS, A_KV_HEADS = 6, 2
B_HEADS = 4
C_HEADS, C_KV_HEADS = 6, 2
C_BRANCHES = ((128, 1), (512, 4), (2048, 16))
CONV_W = 3

A_W = A_HEADS * HEAD_DIM
KV_W = A_KV_HEADS * HEAD_DIM
B_W = B_HEADS * HEAD_DIM
C_W = C_HEADS * HEAD_DIM
OFF_AQ, OFF_AK, OFF_AV = 0, A_W, A_W + KV_W
OFF_B = A_W + 2 * KV_W
OFF_CQ = OFF_B + 5 * B_W
OFF_CK, OFF_CV = OFF_CQ + C_W, OFF_CQ + C_W + KV_W
IN_CUT = OFF_CQ - KV_W
assert OFF_AV % 256 == 0 and IN_CUT % 256 == 0 and (OFF_CV + KV_W) % 256 == 0

TM_PROJ = 512
TQ_ATTN = 256
ATTN_AHEAD = 3
HG_CHUNK = 64
HG_GROUP = 4
HG_BLK = 8
HG_MID = 4
MXU_W = 256
TN_FFN = 1408
TN_ADA = 1536
VMEM_LIMIT = 56 * 1024 * 1024


def _cparams(*sem):
    return pltpu.CompilerParams(dimension_semantics=sem, vmem_limit_bytes=VMEM_LIMIT)


def _dot(a, b):
    return jnp.dot(a, b, preferred_element_type=F32)


def _dot_nt(a, b):
    return lax.dot_general(a, b, (((1,), (1,)), ((), ())), preferred_element_type=F32)


def _dot_tn(a, b):
    return lax.dot_general(a, b, (((0,), (0,)), ((), ())), preferred_element_type=F32)


def _split2(x):
    hi = x.astype(BF16)
    lo = (x - hi.astype(F32)).astype(BF16)
    return hi, lo


def _head_mean_sq(x, bd):
    hi, lo = _split2(x * x)
    return (_dot(hi, bd) + _dot(lo, bd)) * (1.0 / HEAD_DIM)


def _silu(x):
    return x * jax.nn.sigmoid(x)


def _ada_kernel(c_ref, w_ref, b_ref, o_ref):
    a = _silu(c_ref[...]).astype(BF16)
    o_ref[...] = _dot(a, w_ref[...].astype(BF16)) + b_ref[...]


def _ada(c, w_ada, b_ada):
    depth, d, n = w_ada.shape
    bsz = c.shape[0]
    return pl.pallas_call(
        _ada_kernel,
        grid=(depth, n // TN_ADA),
        in_specs=[
            pl.BlockSpec((bsz, d), lambda l, j: (0, 0)),
            pl.BlockSpec((None, d, TN_ADA), lambda l, j: (l, 0, j)),
            pl.BlockSpec((None, 1, TN_ADA), lambda l, j: (l, 0, j)),
        ],
        out_specs=pl.BlockSpec((None, bsz, TN_ADA), lambda l, j: (l, 0, j)),
        out_shape=jax.ShapeDtypeStruct((depth, bsz, n), F32),
        compiler_params=_cparams("arbitrary", "arbitrary"),
        name="ada",
    )(c, w_ada, b_ada.reshape(depth, 1, n))


def _lb_kernel(b_ref, o_ref):
    depth = b_ref.shape[1]
    for d in range(2):
        rows = [b_ref[d, l:l + 1, :] for l in range(depth)]
        m = functools.reduce(jnp.maximum, rows)
        e = [jnp.exp(r - m) for r in rows]
        tot = functools.reduce(lambda a, b: a + b, e)
        run = jnp.zeros_like(m)
        for l in range(depth):
            sm = e[l] / tot
            run = run + sm
            o_ref[d, l:l + 1, :] = run - e[0] / tot


def _hgrn_lb(b_lb):
    return pl.pallas_call(
        _lb_kernel,
        out_shape=jax.ShapeDtypeStruct(b_lb.shape, F32),
        name="hgrn_lb",
    )(b_lb)


def _rope(z, cos, sin):
    n = z.shape[-1]
    lane = lax.broadcasted_iota(jnp.int32, z.shape, 1)
    up = pltpu.roll(z, n - 16, 1)
    dn = pltpu.roll(z, 16, 1)
    return z * cos + jnp.where((lane % 32) < 16, up, dn) * sin


def _in_proj_kernel(x_ref, sh_ref, sc_ref, g_ref, w_ref, cos_ref, sin_ref,
                    gaq_ref, gak_ref, gcq_ref, gck_ref, bd_ref,
                    aq_ref, akt_ref, av_ref, bmix_ref, cq_ref, ck_ref, cv_ref):
    def headnorm(z, gain):
        n = z.shape[-1]
        return z * lax.rsqrt(_head_mean_sq(z, bd_ref[:n, :n]) + EPS) * gain

    x = x_ref[...]
    y = x * lax.rsqrt(jnp.mean(x * x, axis=-1, keepdims=True) + EPS) * g_ref[...]
    h = (y * (1.0 + sc_ref[...]) + sh_ref[...]).astype(BF16)

    cuts = (0, OFF_AV, IN_CUT, OFF_CV + KV_W)
    proj = lambda i: _dot(h, w_ref[:, cuts[i]:cuts[i + 1]])
    cos, sin = cos_ref[...], sin_ref[...]
    d0 = proj(0)
    d1 = proj(1)
    aq_ref[...] = _rope(headnorm(d0[:, :A_W], gaq_ref[...]), cos, sin).astype(BF16)
    ak = _rope(headnorm(d0[:, A_W:], gak_ref[...]), cos[:, :KV_W], sin[:, :KV_W])
    akt_ref[...] = ak.T.astype(BF16)
    d2 = proj(2)
    av_ref[...] = d1[:, :KV_W].astype(BF16)
    bmix_ref[:, :IN_CUT - OFF_B] = d1[:, KV_W:]
    bmix_ref[:, IN_CUT - OFF_B:] = d2[:, :OFF_CQ - IN_CUT]
    c0 = OFF_CQ - IN_CUT
    cq_ref[...] = headnorm(d2[:, c0:c0 + C_W], gcq_ref[...])
    ck_ref[...] = headnorm(d2[:, c0 + C_W:c0 + C_W + KV_W], gck_ref[...])
    cv_ref[...] = d2[:, c0 + C_W + KV_W:]


def _in_proj(x2, mod3, g, w, cos, sin, gaq, gak, gcq, gck, bd, seq):
    t, d = x2.shape
    tm = TM_PROJ
    per_seq = seq // tm
    row = lambda i: (i, 0)
    const = lambda i: (0, 0)
    modspec = lambda k: pl.BlockSpec((None, 1, d), lambda i: (i // per_seq, 0, k))
    return pl.pallas_call(
        _in_proj_kernel,
        grid=(t // tm,),
        in_specs=[
            pl.BlockSpec((tm, d), row),
            modspec(0), modspec(1),
            pl.BlockSpec((1, d), const),
            pl.BlockSpec(w.shape, const),
            pl.BlockSpec((tm, A_W), lambda i: (i % per_seq, 0)),
            pl.BlockSpec((tm, A_W), lambda i: (i % per_seq, 0)),
            pl.BlockSpec((1, A_W), const), pl.BlockSpec((1, KV_W), const),
            pl.BlockSpec((1, C_W), const), pl.BlockSpec((1, KV_W), const),
            pl.BlockSpec(bd.shape, const),
        ],
        out_specs=[
            pl.BlockSpec((tm, A_W), row),
            pl.BlockSpec((KV_W, tm), lambda i: (0, i)),
            pl.BlockSpec((tm, KV_W), row),
            pl.BlockSpec((tm, 5 * B_W), row),
            pl.BlockSpec((tm, C_W), row),
            pl.BlockSpec((tm, KV_W), row),
            pl.BlockSpec((tm, KV_W), row),
        ],
        out_shape=[
            jax.ShapeDtypeStruct((t, A_W), BF16),
            jax.ShapeDtypeStruct((KV_W, t), BF16),
            jax.ShapeDtypeStruct((t, KV_W), BF16),
            jax.ShapeDtypeStruct((t, 5 * B_W), F32),
            jax.ShapeDtypeStruct((t, C_W), F32),
            jax.ShapeDtypeStruct((t, KV_W), F32),
            jax.ShapeDtypeStruct((t, KV_W), F32),
        ],
        compiler_params=_cparams("arbitrary"),
        name="in_proj",
    )(x2, mod3, mod3, g, w, cos, sin, gaq, gak, gcq, gck, bd)


def _alibi_slopes():
    return [float(s) for s in (2.0 ** (-8.0 * np.arange(1, C_HEADS + 1) / C_HEADS)).astype(np.float32)]


def _run_ahead(items, start, finish, ahead=ATTN_AHEAD):
    pending = {}
    for k in range(len(items) + ahead):
        if k < len(items):
            pending[k] = start(items[k])
        if k >= ahead:
            finish(items[k - ahead], pending.pop(k - ahead))


def _attn_kernel(q_ref, kt_ref, v_ref, o_ref, *, n_heads, n_kv):
    group = n_heads // n_kv
    outs = []

    def scores(h):
        j = h // group
        return _dot(q_ref[:, h * HEAD_DIM:(h + 1) * HEAD_DIM], kt_ref[j * HEAD_DIM:(j + 1) * HEAD_DIM, :])

    def finish(h, s):
        j = h // group
        m = jnp.max(s, axis=-1, keepdims=True)
        p = jnp.exp(s - m)
        l = jnp.sum(p, axis=-1, keepdims=True)
        o = _dot(p.astype(BF16), v_ref[...])[:, j * HEAD_DIM:(j + 1) * HEAD_DIM]
        outs.append(o / l)

    _run_ahead(list(range(n_heads)), scores, finish)
    o_ref[...] = jnp.concatenate(outs, axis=1).astype(o_ref.dtype)


def _attention(q, kt, v, seq, *, n_heads, n_kv):
    t, qw = q.shape
    bsz = t // seq
    tq = TQ_ATTN
    per_seq = seq // tq
    return pl.pallas_call(
        functools.partial(_attn_kernel, n_heads=n_heads, n_kv=n_kv),
        grid=(bsz, per_seq),
        in_specs=[
            pl.BlockSpec((tq, qw), lambda b, i: (b * per_seq + i, 0)),
            pl.BlockSpec((kt.shape[0], seq), lambda b, i: (0, b)),
            pl.BlockSpec((seq, v.shape[1]), lambda b, i: (b, 0)),
        ],
        out_specs=pl.BlockSpec((tq, qw), lambda b, i: (b * per_seq + i, 0)),
        out_shape=jax.ShapeDtypeStruct((t, qw), BF16),
        compiler_params=_cparams("arbitrary", "arbitrary"),
        name="attn_rope",
    )(q, kt, v)


DIL_RES = 4
DIL_TQ = 128
DIL_WIN = 3 * DIL_TQ
PAIR_W = 2 * HEAD_DIM
assert C_BRANCHES == ((128, 1), (512, 4), (2048, 16)) and C_BRANCHES[0][0] // 2 <= DIL_TQ


def _dilated_bias_tables(seq):
    slopes = jnp.asarray(_alibi_slopes(), F32).reshape(C_HEADS, 1, 1)
    n = seq // DIL_RES
    i = jnp.arange(n, dtype=jnp.int32)
    a = jnp.abs(i[None, :] - i[:, None])
    w2, w3 = C_BRANCHES[1][0] // (2 * DIL_RES), C_BRANCHES[2][0] // (2 * DIL_RES)
    step3 = C_BRANCHES[2][1] // DIL_RES
    count = (a <= w2).astype(jnp.int32) + ((a <= w3) & (a % step3 == 0)).astype(jnp.int32)
    log_mult = jnp.where(count == 2, float(np.log(2.0)), jnp.where(count == 1, 0.0, NEG_BIG))
    strided = log_mult[None] - slopes * (DIL_RES * a).astype(F32)[None]
    r = jnp.arange(DIL_TQ, dtype=jnp.int32)[:, None]
    c = jnp.arange(DIL_WIN, dtype=jnp.int32)[None, :]
    half = C_BRANCHES[0][0] // 2
    local = []
    for shift in (0, DIL_TQ, 2 * DIL_TQ):
        d = jnp.abs(c - shift - r)
        local.append(jnp.where((d <= half)[None], -slopes * d.astype(F32)[None], NEG_BIG))
    return strided, jnp.stack(local)


def _kv_variants(kt, v_f32):
    zeros = jnp.zeros((HEAD_DIM, kt.shape[1]), BF16)
    ktz = [[jnp.concatenate([kt[j * HEAD_DIM:(j + 1) * HEAD_DIM], zeros], axis=0) for j in range(2)],
           [jnp.concatenate([zeros, kt[j * HEAD_DIM:(j + 1) * HEAD_DIM]], axis=0) for j in range(2)]]
    v = v_f32.astype(BF16)
    vswap = pltpu.roll(v_f32, HEAD_DIM, 1).astype(BF16)
    vsel = [[v, vswap], [vswap, v]]
    return ktz, vsel


def _dil_strided_kernel(q0_ref, q1_ref, q2_ref, k_ref, v_ref, bias_ref,
                        o0_ref, o1_ref, o2_ref, l0_ref, l1_ref, l2_ref):
    n = k_ref.shape[0] // DIL_RES
    rows = pl.ds(pl.program_id(1), n, stride=DIL_RES)
    group = C_HEADS // C_KV_HEADS
    ktz, vsel = _kv_variants(k_ref[rows, :].T.astype(BF16), v_ref[rows, :])
    left = lax.broadcasted_iota(jnp.int32, (n, PAIR_W), 1) < HEAD_DIM
    slabs = ((q0_ref, o0_ref, l0_ref), (q1_ref, o1_ref, l1_ref), (q2_ref, o2_ref, l2_ref))
    qs = [q_ref[rows, :].astype(BF16) for q_ref, _, _ in slabs]
    res, lse = [], []

    def scores(h):
        return _dot(qs[h // 2], ktz[h % 2][h // group]) + bias_ref[h]

    def finish(h, s):
        m = jnp.max(s, axis=-1, keepdims=True)
        p = jnp.exp(s - m)
        l = jnp.sum(p, axis=-1, keepdims=True)
        res.append(_dot(p.astype(BF16), vsel[h % 2][h // group]) / l)
        lse.append(m + jnp.log(l))
        if h % 2 == 1:
            _, o_ref, l_ref = slabs[h // 2]
            o_ref[rows, :] = jnp.where(left, res[-2], res[-1])
            l_ref[rows, :] = jnp.where(left, lse[-2], lse[-1])

    _run_ahead(list(range(C_HEADS)), scores, finish)


def _dil_strided(cq, ck, cv, bias, seq):
    t = cq.shape[0]
    bsz = t // seq
    blk = lambda col: pl.BlockSpec((seq, PAIR_W), lambda b, r: (b, col))
    slab = jax.ShapeDtypeStruct((t, PAIR_W), F32)
    return pl.pallas_call(
        _dil_strided_kernel,
        grid=(bsz, DIL_RES),
        in_specs=[blk(0), blk(1), blk(2), blk(0), blk(0),
                  pl.BlockSpec(bias.shape, lambda b, r: (0, 0, 0))],
        out_specs=[blk(0)] * 6,
        out_shape=[slab] * 6,
        compiler_params=_cparams("arbitrary", "arbitrary"),
        name="attn_dil_strided",
    )(cq, cq, cq, ck, cv, bias)


def _dil_local_kernel(q0_ref, q1_ref, q2_ref, k_ref, v_ref, bias_ref,
                      f0_ref, f1_ref, f2_ref, l0_ref, l1_ref, l2_ref, o_ref):
    seq = k_ref.shape[0]
    tq, win = DIL_TQ, DIL_WIN
    group = C_HEADS // C_KV_HEADS
    ktz, vsel = _kv_variants(k_ref[...].T.astype(BF16), v_ref[...])
    left = lax.broadcasted_iota(jnp.int32, (tq, PAIR_W), 1) < HEAD_DIM
    slabs = ((q0_ref, f0_ref, l0_ref), (q1_ref, f1_ref, l1_ref), (q2_ref, f2_ref, l2_ref))
    items = [(i, slab, half) for i in range(seq // tq) for slab in range(3) for half in range(2)]

    def window(i):
        t0 = i * tq
        w0 = min(max(t0 - tq, 0), seq - win)
        return t0, w0, (t0 - w0) // tq

    def scores(item):
        i, slab, half = item
        t0, w0, variant = window(i)
        h = 2 * slab + half
        q = slabs[slab][0][t0:t0 + tq, :].astype(BF16)
        return _dot(q, ktz[half][h // group][:, w0:w0 + win]) + bias_ref[variant, h]

    def finish(item, s, acc):
        i, slab, half = item
        t0, w0, _ = window(i)
        m = jnp.max(s, axis=-1, keepdims=True)
        p = jnp.exp(s - m)
        near = _dot(p.astype(BF16), vsel[half][(2 * slab + half) // group][w0:w0 + win, :])
        acc.append((near, m, jnp.sum(p, axis=-1, keepdims=True)))
        if half == 0:
            return
        (n0, m0, s0), (n1, m1, s1) = acc[-2], acc[-1]
        near, m_near, l_near = jnp.where(left, n0, n1), jnp.where(left, m0, m1), jnp.where(left, s0, s1)
        _, far_ref, lse_ref = slabs[slab]
        lse_far = lse_ref[t0:t0 + tq, :]
        top = jnp.maximum(m_near, lse_far)
        w_near = jnp.exp(m_near - top)
        w_far = jnp.exp(lse_far - top)
        merged = (near * w_near + far_ref[t0:t0 + tq, :] * w_far) / (l_near * w_near + w_far)
        o_ref[t0:t0 + tq, slab * PAIR_W:(slab + 1) * PAIR_W] = merged.astype(o_ref.dtype)

    acc = []
    _run_ahead(items, scores, lambda item, s: finish(item, s, acc))


def _dil_local(cq, ck, cv, bias, far, seq):
    t = cq.shape[0]
    blk = lambda col: pl.BlockSpec((seq, PAIR_W), lambda b: (b, col))
    return pl.pallas_call(
        _dil_local_kernel,
        grid=(t // seq,),
        in_specs=[blk(0), blk(1), blk(2), blk(0), blk(0),
                  pl.BlockSpec(bias.shape, lambda b: (0, 0, 0, 0))] + [blk(0)] * 6,
        out_specs=pl.BlockSpec((seq, C_W), lambda b: (b, 0)),
        out_shape=jax.ShapeDtypeStruct((t, C_W), BF16),
        compiler_params=_cparams("arbitrary"),
        name="attn_dil_local",
    )(cq, cq, cq, ck, cv, bias, *far)


def _hgrn_tri_constants():
    t = np.arange(HG_GROUP * HG_CHUNK)
    mid = HG_BLK * (t // HG_BLK) + HG_MID
    same_chunk = (t[None, :] // HG_CHUNK) == (t[:, None] // HG_CHUNK)
    lower = (t[None, :] <= t[:, None]) & same_chunk
    upper = (t[None, :] >= t[:, None]) & same_chunk
    fwd = np.concatenate([lower, lower[mid]], axis=0)
    bwd = np.concatenate([upper, upper[mid]], axis=0)
    return jnp.asarray(np.stack([fwd, bwd]).astype(np.float32), dtype=BF16)


def _split3_cols(x):
    hi = x.astype(BF16)
    r = x - hi.astype(F32)
    mid = r.astype(BF16)
    lo = (r - mid.astype(F32)).astype(BF16)
    return jnp.concatenate([hi, mid, lo], axis=1)


def _hgrn_prefix(q, fpre, v, lb, tri, reverse):
    rows, w = q.shape
    c = HG_CHUNK
    f = jnp.maximum(lb + (1.0 - lb) * jax.nn.sigmoid(fpre), F_MIN)
    lf = jnp.log(f)
    kk = 1.0 - f
    r = _dot(tri, _split3_cols(lf))
    cum = r[:, 2 * w:] + r[:, w:2 * w] + r[:, :w]
    b, bmid = cum[:rows], cum[rows:]
    blast = [b[g * c:g * c + 1] if reverse else b[(g + 1) * c - 1:(g + 1) * c] for g in range(rows // c)]
    blast_rows = jnp.concatenate([jnp.broadcast_to(x, (c, w)) for x in blast], axis=0)
    return dict(q=q, b=b, blast=blast, vb=v.astype(BF16), reverse=reverse,
                qe=(q * jnp.exp(b)).astype(BF16),
                kdec=(kk * jnp.exp(blast_rows - b)).astype(BF16),
                ktil=kk * jnp.exp(bmid - b))


def _hgrn_tilde(pre, rs, sl, blk_mask):
    qh, bh, reverse = pre["q"][rs, sl], pre["b"][rs, sl], pre["reverse"]
    c = qh.shape[0]
    nblk = c // HG_BLK
    pieces = []
    for j in range(nblk):
        ref_row = bh[j * HG_BLK + HG_MID:j * HG_BLK + HG_MID + 1, :]
        if reverse:
            hi_row = (j + 1) * HG_BLK
            qt = qh[:hi_row] * jnp.exp(bh[:hi_row] - ref_row)
            if hi_row < c:
                qt = jnp.concatenate([qt, jnp.zeros((c - hi_row, HEAD_DIM), F32)], axis=0)
        else:
            lo_row = j * HG_BLK
            qt = qh[lo_row:] * jnp.exp(bh[lo_row:] - ref_row)
            if lo_row > 0:
                qt = jnp.concatenate([jnp.zeros((lo_row, HEAD_DIM), F32), qt], axis=0)
        pieces.append(qt)
    qtil = jnp.concatenate(pieces, axis=1).astype(BF16)
    kt = jnp.concatenate([pre["ktil"][rs, sl]] * nblk, axis=1)
    return qtil, jnp.where(blk_mask, kt, 0.0).astype(BF16)


def _hgrn_kernel(qf_ref, ff_ref, vf_ref, qb_ref, fb_ref, vb_ref, lbf_ref, lbb_ref, tri_ref,
                 of_ref, ob_ref, sf_ref, sb_ref):
    @pl.when(pl.program_id(1) == 0)
    def _():
        sf_ref[...] = jnp.zeros_like(sf_ref)
        sb_ref[...] = jnp.zeros_like(sb_ref)

    c = HG_CHUNK
    ngrp = qf_ref.shape[0] // c
    nblk = c // HG_BLK
    row = lax.broadcasted_iota(jnp.int32, (c, c), 0)
    col = lax.broadcasted_iota(jnp.int32, (c, c), 1)
    brow = lax.broadcasted_iota(jnp.int32, (c, nblk * HEAD_DIM), 0) // HG_BLK
    bcol = lax.broadcasted_iota(jnp.int32, (c, nblk * HEAD_DIM), 1) // HEAD_DIM
    blk_mask = brow == bcol

    pres = [_hgrn_prefix(qf_ref[...], ff_ref[...], vf_ref[...], lbf_ref[...], tri_ref[0], False),
            _hgrn_prefix(qb_ref[...], fb_ref[...], vb_ref[...], lbb_ref[...], tri_ref[1], True)]
    order = [list(range(ngrp)), list(range(ngrp - 1, -1, -1))]
    units = [(d, g, h) for d in range(2) for g in order[d] for h in range(B_HEADS)]
    sl = lambda h: slice(h * HEAD_DIM, (h + 1) * HEAD_DIM)
    rs = lambda g: slice(g * c, (g + 1) * c)
    state_refs = (sf_ref, sb_ref)

    upd = {u: _dot_tn(pres[u[0]]["vb"][rs(u[1]), sl(u[2])], pres[u[0]]["kdec"][rs(u[1]), sl(u[2])])
           for u in units}
    attn = {}
    for d, g, h in units:
        qtil, kt = _hgrn_tilde(pres[d], rs(g), sl(h), blk_mask)
        attn[d, g, h] = _dot_nt(qtil, kt)
    inter = {}
    for d in range(2):
        for h in range(B_HEADS):
            st = state_refs[d][h]
            for g in order[d]:
                inter[d, g, h] = _dot_nt(pres[d]["qe"][rs(g), sl(h)], st.astype(BF16))
                st = st * jnp.exp(pres[d]["blast"][g][:, sl(h)]) + upd[d, g, h]
            state_refs[d][h] = st
    outs = [[[None] * B_HEADS for _ in range(ngrp)] for _ in range(2)]
    for d, g, h in units:
        keep = (col >= row) if pres[d]["reverse"] else (col <= row)
        a = jnp.where(keep, attn[d, g, h], 0.0).astype(BF16)
        outs[d][g][h] = _dot(a, pres[d]["vb"][rs(g), sl(h)]) + inter[d, g, h]
    for d, o_ref in enumerate((of_ref, ob_ref)):
        o_ref[...] = jnp.concatenate([jnp.concatenate(per_head, axis=1) for per_head in outs[d]], axis=0)


def _hgrn(bmix, lbf, lbb, tri, seq):
    t = bmix.shape[0]
    bsz = t // seq
    c = HG_GROUP * HG_CHUNK
    n = seq // c
    fwd = lambda k: pl.BlockSpec((c, B_W), lambda b, j: (b * n + j, k))
    bwd = lambda k: pl.BlockSpec((c, B_W), lambda b, j: (b * n + n - 1 - j, k))
    const2 = lambda b, j: (0, 0)
    return pl.pallas_call(
        _hgrn_kernel,
        grid=(bsz, n),
        in_specs=[fwd(0), fwd(1), fwd(3), bwd(0), bwd(2), bwd(3),
                  pl.BlockSpec((1, B_W), const2), pl.BlockSpec((1, B_W), const2),
                  pl.BlockSpec(tri.shape, lambda b, j: (0, 0, 0))],
        out_specs=[fwd(0), bwd(0)],
        out_shape=[jax.ShapeDtypeStruct((t, B_W), F32), jax.ShapeDtypeStruct((t, B_W), F32)],
        scratch_shapes=[pltpu.VMEM((B_HEADS, HEAD_DIM, HEAD_DIM), F32),
                        pltpu.VMEM((B_HEADS, HEAD_DIM, HEAD_DIM), F32)],
        compiler_params=_cparams("arbitrary", "arbitrary"),
        name="hgrn",
    )(bmix, bmix, bmix, bmix, bmix, bmix, lbf, lbb, tri)


def _out_proj_kernel(x_ref, oa_ref, of_ref, ob_ref, bg_ref, oc_ref, w_ref, gb_ref, bd_ref,
                     g1_ref, sh_ref, sc_ref, g_ref, xo_ref, h_ref):
    ob = of_ref[...] + ob_ref[...]
    ob = ob * lax.rsqrt(_head_mean_sq(ob, bd_ref[...]) + EPS) * gb_ref[...]
    ob = (ob * _silu(bg_ref[...])).astype(BF16)
    mix = (_dot(oa_ref[...], w_ref[:A_W, :]) + _dot(ob, w_ref[A_W:A_W + B_W, :])
           + _dot(oc_ref[...], w_ref[A_W + B_W:, :]))
    x = x_ref[...] + g1_ref[...] * mix
    xo_ref[...] = x
    y = x * lax.rsqrt(jnp.mean(x * x, axis=-1, keepdims=True) + EPS) * g_ref[...]
    h_ref[...] = (y * (1.0 + sc_ref[...]) + sh_ref[...]).astype(BF16)


def _out_proj(x2, oa, of, ob, bmix, oc, w, gb, bd, mod3, g, seq):
    t, d = x2.shape
    tm = TM_PROJ
    per_seq = seq // tm
    row = lambda i: (i, 0)
    const = lambda i: (0, 0)
    modspec = lambda k: pl.BlockSpec((None, 1, d), lambda i: (i // per_seq, 0, k))
    return pl.pallas_call(
        _out_proj_kernel,
        grid=(t // tm,),
        in_specs=[
            pl.BlockSpec((tm, d), row),
            pl.BlockSpec((tm, A_W), row),
            pl.BlockSpec((tm, B_W), row), pl.BlockSpec((tm, B_W), row),
            pl.BlockSpec((tm, B_W), lambda i: (i, 4)),
            pl.BlockSpec((tm, C_W), row),
            pl.BlockSpec(w.shape, const),
            pl.BlockSpec((1, B_W), const),
            pl.BlockSpec(bd.shape, const),
            modspec(2), modspec(3), modspec(4),
            pl.BlockSpec((1, d), const),
        ],
        out_specs=[pl.BlockSpec((tm, d), row), pl.BlockSpec((tm, d), row)],
        out_shape=[jax.ShapeDtypeStruct((t, d), F32), jax.ShapeDtypeStruct((t, d), BF16)],
        compiler_params=_cparams("arbitrary"),
        name="out_proj",
    )(x2, oa, of, ob, bmix, oc, w, gb, bd, mod3, mod3, mod3, g)


def _ffn_up_kernel(h_ref, wa_ref, wb_ref, cwa_ref, cwb_ref, cba_ref, cbb_ref, o_ref):
    h = h_ref[...]
    seq = h.shape[0]
    tn = o_ref.shape[1]
    row = lax.broadcasted_iota(jnp.int32, (seq, 1), 0)
    subs = [(off, min(MXU_W, tn - off)) for off in range(0, tn, MXU_W)]

    def matmuls(sub):
        off, w = sub
        if w == MXU_W:
            return _dot(h, wa_ref[:, off:off + w]), _dot(h, wb_ref[:, off:off + w])
        u = _dot(h, jnp.concatenate([wa_ref[:, off:off + w], wb_ref[:, off:off + w]], axis=1))
        return u[:, :w], u[:, w:]

    def conv(u, cw, cb):
        prev = jnp.where(row == 0, 0.0, pltpu.roll(u, 1, 0))
        nxt = jnp.where(row == seq - 1, 0.0, pltpu.roll(u, seq - 1, 0))
        return cw[0:1, :] * prev + cw[1:2, :] * u + cw[2:3, :] * nxt + cb

    def gate(sub, us):
        off, w = sub
        cols = slice(off, off + w)
        a = conv(us[0], cwa_ref[:, cols], cba_ref[:, cols])
        b = conv(us[1], cwb_ref[:, cols], cbb_ref[:, cols])
        o_ref[:, cols] = (_silu(a) * b).astype(BF16)

    _run_ahead(subs, matmuls, gate, ahead=2)


def _ffn_up(h, w_up, conv_w, conv_b, seq):
    t, d = h.shape
    dff = w_up.shape[1] // 2
    tn = TN_FFN
    nt = dff // tn
    lo = lambda b, j: (0, j)
    hi = lambda b, j: (0, nt + j)
    return pl.pallas_call(
        _ffn_up_kernel,
        grid=(t // seq, nt),
        in_specs=[
            pl.BlockSpec((seq, d), lambda b, j: (b, 0)),
            pl.BlockSpec((d, tn), lo), pl.BlockSpec((d, tn), hi),
            pl.BlockSpec((CONV_W, tn), lo), pl.BlockSpec((CONV_W, tn), hi),
            pl.BlockSpec((1, tn), lo), pl.BlockSpec((1, tn), hi),
        ],
        out_specs=pl.BlockSpec((seq, tn), lambda b, j: (b, j)),
        out_shape=jax.ShapeDtypeStruct((t, dff), BF16),
        compiler_params=_cparams("arbitrary", "arbitrary"),
        name="ffn_up",
    )(h, w_up, w_up, conv_w, conv_w, conv_b, conv_b)


def _ffn_down_kernel(x_ref, a_ref, w_ref, g2_ref, o_ref):
    o_ref[...] = x_ref[...] + g2_ref[...] * _dot(a_ref[...], w_ref[...])


def _ffn_down(x2, act, w, mod3, seq):
    t, d = x2.shape
    tm = TM_PROJ
    per_seq = seq // tm
    row = lambda i: (i, 0)
    return pl.pallas_call(
        _ffn_down_kernel,
        grid=(t // tm,),
        in_specs=[
            pl.BlockSpec((tm, d), row),
            pl.BlockSpec((tm, act.shape[1]), row),
            pl.BlockSpec(w.shape, lambda i: (0, 0)),
            pl.BlockSpec((None, 1, d), lambda i: (i // per_seq, 0, 5)),
        ],
        out_specs=pl.BlockSpec((tm, d), row),
        out_shape=jax.ShapeDtypeStruct((t, d), F32),
        compiler_params=_cparams("arbitrary"),
        name="ffn_down",
    )(x2, act, w, mod3)


def _rope_tables(seq):
    n_rows = seq // GRID_W
    rowp = np.repeat(np.arange(n_rows), GRID_W).astype(np.float32)
    colp = np.tile(np.arange(GRID_W), n_rows).astype(np.float32)
    half = HEAD_DIM // 2
    inv = (np.float32(ROPE_THETA) ** (-np.arange(0, half, 2, dtype=np.float32) / half)).astype(np.float32)
    ang_r = rowp[:, None] * inv
    ang_c = colp[:, None] * inv
    cos = np.concatenate([np.cos(ang_r), np.cos(ang_r), np.cos(ang_c), np.cos(ang_c)], axis=1)
    sin = np.concatenate([-np.sin(ang_r), np.sin(ang_r), -np.sin(ang_c), np.sin(ang_c)], axis=1)
    tile = lambda a: jnp.asarray(np.tile(a.astype(np.float32), (1, A_HEADS)))
    return tile(cos), tile(sin)


def _head_block_ones(width):
    i = np.arange(width) // HEAD_DIM
    return jnp.asarray((i[:, None] == i[None, :]).astype(np.float32), dtype=BF16)


def kernel(x, c, w_ada, b_ada, norm_g, w_in, a_q_norm, a_k_norm, b_lb, b_out_norm, c_q_norm, c_k_norm,
           w_out, w_up, conv_w, conv_b, w_down):
    bsz, seq, d = x.shape
    depth = w_in.shape[0]
    t = bsz * seq
    scale = HEAD_DIM ** -0.5

    mod = _ada(c, w_ada, b_ada)
    lb_all = _hgrn_lb(b_lb.astype(F32))
    cos, sin = _rope_tables(seq)
    bd = _head_block_ones(A_W)
    tri = _hgrn_tri_constants()
    bias_strided, bias_local = _dilated_bias_tables(seq)
    tile = lambda gvec, heads, s=1.0: (jnp.tile(gvec.astype(F32), heads) * s).reshape(1, -1)

    x2 = x.reshape(t, d)
    for l in range(depth):
        mod3 = mod[l].reshape(bsz, 1, 6 * d)
        aq, akt, av, bmix, cq, ck, cv = _in_proj(
            x2, mod3, norm_g[l, 0].reshape(1, d), w_in[l].astype(BF16), cos, sin,
            tile(a_q_norm[l], A_HEADS, scale), tile(a_k_norm[l], A_KV_HEADS),
            tile(c_q_norm[l], C_HEADS, scale), tile(c_k_norm[l], C_KV_HEADS), bd, seq)
        o_a = _attention(aq, akt, av, seq, n_heads=A_HEADS, n_kv=A_KV_HEADS)
        far = _dil_strided(cq, ck, cv, bias_strided, seq)
        o_c = _dil_local(cq, ck, cv, bias_local, far, seq)
        o_f, o_b = _hgrn(bmix, lb_all[0, l].reshape(1, B_W), lb_all[1, l].reshape(1, B_W), tri, seq)
        x2, h2 = _out_proj(x2, o_a, o_f, o_b, bmix, o_c, w_out[l].astype(BF16),
                           tile(b_out_norm[l], B_HEADS), bd[:B_W, :B_W], mod3,
                           norm_g[l, 1].reshape(1, d), seq)
        act = _ffn_up(h2, w_up[l].astype(BF16), conv_w[l], conv_b[l].reshape(1, -1), seq)
        x2 = _ffn_down(x2, act, w_down[l].astype(BF16), mod3, seq)
    return x2.reshape(bsz, seq, d)
```

```python
import functools

import numpy as np
import jax
import jax.numpy as jnp
from jax import lax
from jax.experimental import pallas as pl
from jax.experimental.pallas import tpu as pltpu

F32 = jnp.float32
BF16 = jnp.bfloat16

HEAD_DIM = 64
GRID_W = 64
EPS = 1e-6
NEG_BIG = -1e30
F_MIN = 1e-6
ROPE_THETA = 10000.0
A_HEADS, A_KV_HEADS = 6, 2
B_HEADS = 4
C_HEADS, C_KV_HEADS = 6, 2
C_BRANCHES = ((128, 1), (512, 4), (2048, 16))
CONV_W = 3

A_W = A_HEADS * HEAD_DIM
KV_W = A_KV_HEADS * HEAD_DIM
B_W = B_HEADS * HEAD_DIM
C_W = C_HEADS * HEAD_DIM
OFF_AQ, OFF_AK, OFF_AV = 0, A_W, A_W + KV_W
OFF_B = A_W + 2 * KV_W
OFF_CQ = OFF_B + 5 * B_W
OFF_CK, OFF_CV = OFF_CQ + C_W, OFF_CQ + C_W + KV_W
IN_CUT = OFF_CQ - KV_W
assert OFF_AV % 256 == 0 and IN_CUT % 256 == 0 and (OFF_CV + KV_W) % 256 == 0

TM_PROJ = 512
TQ_ATTN = 256
KEY_CHUNK = 256
ATTN_AHEAD = 3
HG_CHUNK = 64
HG_GROUP = 4
HG_BLK = 8
HG_MID = 4
MXU_W = 256
TN_FFN = 1408
TN_ADA = 1536
VMEM_LIMIT = 56 * 1024 * 1024


def _cparams(*sem):
    return pltpu.CompilerParams(dimension_semantics=sem, vmem_limit_bytes=VMEM_LIMIT)


def _dot(a, b):
    return jnp.dot(a, b, preferred_element_type=F32)


def _dot_nt(a, b):
    return lax.dot_general(a, b, (((1,), (1,)), ((), ())), preferred_element_type=F32)


def _dot_tn(a, b):
    return lax.dot_general(a, b, (((0,), (0,)), ((), ())), preferred_element_type=F32)


def _split2(x):
    hi = x.astype(BF16)
    lo = (x - hi.astype(F32)).astype(BF16)
    return hi, lo


def _head_mean_sq(x, bd):
    hi, lo = _split2(x * x)
    return (_dot(hi, bd) + _dot(lo, bd)) * (1.0 / HEAD_DIM)


def _silu(x):
    return x * jax.nn.sigmoid(x)


def _ada_kernel(c_ref, w_ref, b_ref, o_ref):
    a = _silu(c_ref[...]).astype(BF16)
    o_ref[...] = _dot(a, w_ref[...].astype(BF16)) + b_ref[...]


def _ada(c, w_ada, b_ada):
    depth, d, n = w_ada.shape
    bsz = c.shape[0]
    return pl.pallas_call(
        _ada_kernel,
        grid=(depth, n // TN_ADA),
        in_specs=[
            pl.BlockSpec((bsz, d), lambda l, j: (0, 0)),
            pl.BlockSpec((None, d, TN_ADA), lambda l, j: (l, 0, j)),
            pl.BlockSpec((None, 1, TN_ADA), lambda l, j: (l, 0, j)),
        ],
        out_specs=pl.BlockSpec((None, bsz, TN_ADA), lambda l, j: (l, 0, j)),
        out_shape=jax.ShapeDtypeStruct((depth, bsz, n), F32),
        compiler_params=_cparams("arbitrary", "arbitrary"),
        name="ada",
    )(c, w_ada, b_ada.reshape(depth, 1, n))


def _lb_kernel(b_ref, o_ref):
    depth = b_ref.shape[1]
    for d in range(2):
        rows = [b_ref[d, l:l + 1, :] for l in range(depth)]
        m = functools.reduce(jnp.maximum, rows)
        e = [jnp.exp(r - m) for r in rows]
        tot = functools.reduce(lambda a, b: a + b, e)
        run = jnp.zeros_like(m)
        for l in range(depth):
            sm = e[l] / tot
            run = run + sm
            o_ref[d, l:l + 1, :] = run - e[0] / tot


def _hgrn_lb(b_lb):
    return pl.pallas_call(
        _lb_kernel,
        out_shape=jax.ShapeDtypeStruct(b_lb.shape, F32),
        name="hgrn_lb",
    )(b_lb)


def _rope(z, cos, sin):
    n = z.shape[-1]
    lane = lax.broadcasted_iota(jnp.int32, z.shape, 1)
    up = pltpu.roll(z, n - 16, 1)
    dn = pltpu.roll(z, 16, 1)
    return z * cos + jnp.where((lane % 32) < 16, up, dn) * sin


def _in_proj_kernel(x_ref, sh_ref, sc_ref, g_ref, w_ref, cos_ref, sin_ref,
                    gaq_ref, gak_ref, gcq_ref, gck_ref, bd_ref,
                    aqt_ref, ak_ref, avt_ref, bmix_ref, cq_ref, ck_ref, cv_ref):
    def headnorm(z, gain):
        n = z.shape[-1]
        return z * lax.rsqrt(_head_mean_sq(z, bd_ref[:n, :n]) + EPS) * gain

    x = x_ref[...]
    y = x * lax.rsqrt(jnp.mean(x * x, axis=-1, keepdims=True) + EPS) * g_ref[...]
    h = (y * (1.0 + sc_ref[...]) + sh_ref[...]).astype(BF16)

    cuts = (0, OFF_AV, IN_CUT, OFF_CV + KV_W)
    proj = lambda i: _dot(h, w_ref[:, cuts[i]:cuts[i + 1]])
    cos, sin = cos_ref[...], sin_ref[...]
    d0 = proj(0)
    d1 = proj(1)
    aqt_ref[...] = _rope(headnorm(d0[:, :A_W], gaq_ref[...]), cos, sin).T.astype(BF16)
    ak_ref[...] = _rope(headnorm(d0[:, A_W:], gak_ref[...]), cos[:, :KV_W], sin[:, :KV_W]).astype(BF16)
    d2 = proj(2)
    avt_ref[...] = d1[:, :KV_W].T.astype(BF16)
    bmix_ref[:, :IN_CUT - OFF_B] = d1[:, KV_W:]
    bmix_ref[:, IN_CUT - OFF_B:] = d2[:, :OFF_CQ - IN_CUT]
    c0 = OFF_CQ - IN_CUT
    cq_ref[...] = headnorm(d2[:, c0:c0 + C_W], gcq_ref[...])
    ck_ref[...] = headnorm(d2[:, c0 + C_W:c0 + C_W + KV_W], gck_ref[...])
    cv_ref[...] = d2[:, c0 + C_W + KV_W:]


def _in_proj(x2, mod3, g, w, cos, sin, gaq, gak, gcq, gck, bd, seq):
    t, d = x2.shape
    tm = TM_PROJ
    per_seq = seq // tm
    row = lambda i: (i, 0)
    const = lambda i: (0, 0)
    modspec = lambda k: pl.BlockSpec((None, 1, d), lambda i: (i // per_seq, 0, k))
    return pl.pallas_call(
        _in_proj_kernel,
        grid=(t // tm,),
        in_specs=[
            pl.BlockSpec((tm, d), row),
            modspec(0), modspec(1),
            pl.BlockSpec((1, d), const),
            pl.BlockSpec(w.shape, const),
            pl.BlockSpec((tm, A_W), lambda i: (i % per_seq, 0)),
            pl.BlockSpec((tm, A_W), lambda i: (i % per_seq, 0)),
            pl.BlockSpec((1, A_W), const), pl.BlockSpec((1, KV_W), const),
            pl.BlockSpec((1, C_W), const), pl.BlockSpec((1, KV_W), const),
            pl.BlockSpec(bd.shape, const),
        ],
        out_specs=[
            pl.BlockSpec((A_W, tm), lambda i: (0, i)),
            pl.BlockSpec((tm, KV_W), row),
            pl.BlockSpec((KV_W, tm), lambda i: (0, i)),
            pl.BlockSpec((tm, 5 * B_W), row),
            pl.BlockSpec((tm, C_W), row),
            pl.BlockSpec((tm, KV_W), row),
            pl.BlockSpec((tm, KV_W), row),
        ],
        out_shape=[
            jax.ShapeDtypeStruct((A_W, t), BF16),
            jax.ShapeDtypeStruct((t, KV_W), BF16),
            jax.ShapeDtypeStruct((KV_W, t), BF16),
            jax.ShapeDtypeStruct((t, 5 * B_W), F32),
            jax.ShapeDtypeStruct((t, C_W), F32),
            jax.ShapeDtypeStruct((t, KV_W), F32),
            jax.ShapeDtypeStruct((t, KV_W), F32),
        ],
        compiler_params=_cparams("arbitrary"),
        name="in_proj",
    )(x2, mod3, mod3, g, w, cos, sin, gaq, gak, gcq, gck, bd)


def _alibi_slopes():
    return [float(s) for s in (2.0 ** (-8.0 * np.arange(1, C_HEADS + 1) / C_HEADS)).astype(np.float32)]


def _run_ahead(items, start, finish, ahead=ATTN_AHEAD):
    pending = {}
    for k in range(len(items) + ahead):
        if k < len(items):
            pending[k] = start(items[k])
        if k >= ahead:
            finish(items[k - ahead], pending.pop(k - ahead))


def _attn_kernel(qt_ref, k_ref, vt_ref, o_ref, s_ref, *, n_heads, n_kv):
    group = n_heads // n_kv
    tq, kc = TQ_ATTN, KEY_CHUNK
    seq = k_ref.shape[0]
    zeros = jnp.zeros((HEAD_DIM, tq), BF16)
    fold = lambda x, op: op(x.reshape(kc // 8, 8, tq), axis=0)
    items = [(i, h) for i in range(qt_ref.shape[1] // tq) for h in range(n_heads)]
    outs = []

    def scores(item):
        i, h = item
        qt = qt_ref[h * HEAD_DIM:(h + 1) * HEAD_DIM, i * tq:(i + 1) * tq]
        qtz = jnp.concatenate([qt, zeros] if h // group == 0 else [zeros, qt], axis=0)
        slot = items.index(item) % s_ref.shape[0]
        for r in range(0, seq, kc):
            s_ref[slot, r:r + kc, :] = _dot(k_ref[r:r + kc, :], qtz)
        return slot

    def finish(item, slot):
        i, h = item
        j = h // group
        m8 = fold(s_ref[slot, 0:kc, :], jnp.max)
        for r in range(kc, seq, kc):
            m8 = jnp.maximum(m8, fold(s_ref[slot, r:r + kc, :], jnp.max))
        m = jnp.max(m8, axis=0, keepdims=True)
        l8 = jnp.zeros((8, tq), F32)
        ot = jnp.zeros((HEAD_DIM, tq), F32)
        for r in range(0, seq, kc):
            p = jnp.exp(s_ref[slot, r:r + kc, :] - m)
            l8 = l8 + fold(p, jnp.sum)
            ot = ot + _dot(vt_ref[j * HEAD_DIM:(j + 1) * HEAD_DIM, r:r + kc], p.astype(BF16))
        outs.append(ot / jnp.sum(l8, axis=0, keepdims=True))
        if h == n_heads - 1:
            o_ref[i * tq:(i + 1) * tq, :] = jnp.concatenate(outs[-n_heads:], axis=0).T.astype(o_ref.dtype)

    _run_ahead(items, scores, finish)


def _attention(qt, k, vt, seq, *, n_heads, n_kv):
    qw, t = qt.shape
    assert n_kv == 2 and k.shape[1] == n_kv * HEAD_DIM
    return pl.pallas_call(
        functools.partial(_attn_kernel, n_heads=n_heads, n_kv=n_kv),
        grid=(t // seq,),
        in_specs=[
            pl.BlockSpec((qw, seq), lambda b: (0, b)),
            pl.BlockSpec((seq, k.shape[1]), lambda b: (b, 0)),
            pl.BlockSpec((vt.shape[0], seq), lambda b: (0, b)),
        ],
        out_specs=pl.BlockSpec((seq, qw), lambda b: (b, 0)),
        out_shape=jax.ShapeDtypeStruct((t, qw), BF16),
        scratch_shapes=[pltpu.VMEM((ATTN_AHEAD + 1, seq, TQ_ATTN), F32)],
        compiler_params=_cparams("arbitrary"),
        name="attn_rope",
    )(qt, k, vt)


DIL_RES = 4
DIL_TQ = 128
DIL_WIN = 3 * DIL_TQ
PAIR_W = 2 * HEAD_DIM
assert C_BRANCHES == ((128, 1), (512, 4), (2048, 16)) and C_BRANCHES[0][0] // 2 <= DIL_TQ


def _dilated_bias_tables(seq):
    slopes = jnp.asarray(_alibi_slopes(), F32).reshape(C_HEADS, 1, 1)
    n = seq // DIL_RES
    i = jnp.arange(n, dtype=jnp.int32)
    a = jnp.abs(i[None, :] - i[:, None])
    w2, w3 = C_BRANCHES[1][0] // (2 * DIL_RES), C_BRANCHES[2][0] // (2 * DIL_RES)
    step3 = C_BRANCHES[2][1] // DIL_RES
    count = (a <= w2).astype(jnp.int32) + ((a <= w3) & (a % step3 == 0)).astype(jnp.int32)
    log_mult = jnp.where(count == 2, float(np.log(2.0)), jnp.where(count == 1, 0.0, NEG_BIG))
    strided = log_mult[None] - slopes * (DIL_RES * a).astype(F32)[None]
    r = jnp.arange(DIL_TQ, dtype=jnp.int32)[:, None]
    c = jnp.arange(DIL_WIN, dtype=jnp.int32)[None, :]
    half = C_BRANCHES[0][0] // 2
    local = []
    for shift in (0, DIL_TQ, 2 * DIL_TQ):
        d = jnp.abs(c - shift - r)
        local.append(jnp.where((d <= half)[None], -slopes * d.astype(F32)[None], NEG_BIG))
    return strided, jnp.stack(local)


def _kv_variants(kt, v_f32):
    zeros = jnp.zeros((HEAD_DIM, kt.shape[1]), BF16)
    ktz = [[jnp.concatenate([kt[j * HEAD_DIM:(j + 1) * HEAD_DIM], zeros], axis=0) for j in range(2)],
           [jnp.concatenate([zeros, kt[j * HEAD_DIM:(j + 1) * HEAD_DIM]], axis=0) for j in range(2)]]
    v = v_f32.astype(BF16)
    vswap = pltpu.roll(v_f32, HEAD_DIM, 1).astype(BF16)
    vsel = [[v, vswap], [vswap, v]]
    return ktz, vsel


def _dil_strided_kernel(q0_ref, q1_ref, q2_ref, k_ref, v_ref, bias_ref,
                        o0_ref, o1_ref, o2_ref, l0_ref, l1_ref, l2_ref):
    n = k_ref.shape[0] // DIL_RES
    rows = pl.ds(pl.program_id(1), n, stride=DIL_RES)
    group = C_HEADS // C_KV_HEADS
    ktz, vsel = _kv_variants(k_ref[rows, :].T.astype(BF16), v_ref[rows, :])
    left = lax.broadcasted_iota(jnp.int32, (n, PAIR_W), 1) < HEAD_DIM
    slabs = ((q0_ref, o0_ref, l0_ref), (q1_ref, o1_ref, l1_ref), (q2_ref, o2_ref, l2_ref))
    qs = [q_ref[rows, :].astype(BF16) for q_ref, _, _ in slabs]
    res, lse = [], []

    def scores(h):
        return _dot(qs[h // 2], ktz[h % 2][h // group]) + bias_ref[h]

    def finish(h, s):
        m = jnp.max(s, axis=-1, keepdims=True)
        p = jnp.exp(s - m)
        l = jnp.sum(p, axis=-1, keepdims=True)
        res.append(_dot(p.astype(BF16), vsel[h % 2][h // group]) / l)
        lse.append(m + jnp.log(l))
        if h % 2 == 1:
            _, o_ref, l_ref = slabs[h // 2]
            o_ref[rows, :] = jnp.where(left, res[-2], res[-1])
            l_ref[rows, :] = jnp.where(left, lse[-2], lse[-1])

    _run_ahead(list(range(C_HEADS)), scores, finish)


def _dil_strided(cq, ck, cv, bias, seq):
    t = cq.shape[0]
    bsz = t // seq
    blk = lambda col: pl.BlockSpec((seq, PAIR_W), lambda b, r: (b, col))
    slab = jax.ShapeDtypeStruct((t, PAIR_W), F32)
    return pl.pallas_call(
        _dil_strided_kernel,
        grid=(bsz, DIL_RES),
        in_specs=[blk(0), blk(1), blk(2), blk(0), blk(0),
                  pl.BlockSpec(bias.shape, lambda b, r: (0, 0, 0))],
        out_specs=[blk(0)] * 6,
        out_shape=[slab] * 6,
        compiler_params=_cparams("arbitrary", "arbitrary"),
        name="attn_dil_strided",
    )(cq, cq, cq, ck, cv, bias)


def _dil_local_kernel(q0_ref, q1_ref, q2_ref, k_ref, v_ref, bias_ref,
                      f0_ref, f1_ref, f2_ref, l0_ref, l1_ref, l2_ref, o_ref):
    seq = k_ref.shape[0]
    tq, win = DIL_TQ, DIL_WIN
    group = C_HEADS // C_KV_HEADS
    ktz, vsel = _kv_variants(k_ref[...].T.astype(BF16), v_ref[...])
    left = lax.broadcasted_iota(jnp.int32, (tq, PAIR_W), 1) < HEAD_DIM
    slabs = ((q0_ref, f0_ref, l0_ref), (q1_ref, f1_ref, l1_ref), (q2_ref, f2_ref, l2_ref))
    items = [(i, slab, half) for i in range(seq // tq) for slab in range(3) for half in range(2)]

    def window(i):
        t0 = i * tq
        w0 = min(max(t0 - tq, 0), seq - win)
        return t0, w0, (t0 - w0) // tq

    def scores(item):
        i, slab, half = item
        t0, w0, variant = window(i)
        h = 2 * slab + half
        q = slabs[slab][0][t0:t0 + tq, :].astype(BF16)
        return _dot(q, ktz[half][h // group][:, w0:w0 + win]) + bias_ref[variant, h]

    def finish(item, s, acc):
        i, slab, half = item
        t0, w0, _ = window(i)
        m = jnp.max(s, axis=-1, keepdims=True)
        p = jnp.exp(s - m)
        near = _dot(p.astype(BF16), vsel[half][(2 * slab + half) // group][w0:w0 + win, :])
        acc.append((near, m, jnp.sum(p, axis=-1, keepdims=True)))
        if half == 0:
            return
        (n0, m0, s0), (n1, m1, s1) = acc[-2], acc[-1]
        near, m_near, l_near = jnp.where(left, n0, n1), jnp.where(left, m0, m1), jnp.where(left, s0, s1)
        _, far_ref, lse_ref = slabs[slab]
        lse_far = lse_ref[t0:t0 + tq, :]
        top = jnp.maximum(m_near, lse_far)
        w_near = jnp.exp(m_near - top)
        w_far = jnp.exp(lse_far - top)
        merged = (near * w_near + far_ref[t0:t0 + tq, :] * w_far) / (l_near * w_near + w_far)
        o_ref[t0:t0 + tq, slab * PAIR_W:(slab + 1) * PAIR_W] = merged.astype(o_ref.dtype)

    acc = []
    _run_ahead(items, scores, lambda item, s: finish(item, s, acc))


def _dil_local(cq, ck, cv, bias, far, seq):
    t = cq.shape[0]
    blk = lambda col: pl.BlockSpec((seq, PAIR_W), lambda b: (b, col))
    return pl.pallas_call(
        _dil_local_kernel,
        grid=(t // seq,),
        in_specs=[blk(0), blk(1), blk(2), blk(0), blk(0),
                  pl.BlockSpec(bias.shape, lambda b: (0, 0, 0, 0))] + [blk(0)] * 6,
        out_specs=pl.BlockSpec((seq, C_W), lambda b: (b, 0)),
        out_shape=jax.ShapeDtypeStruct((t, C_W), BF16),
        compiler_params=_cparams("arbitrary"),
        name="attn_dil_local",
    )(cq, cq, cq, ck, cv, bias, *far)


def _hgrn_tri_constants():
    t = np.arange(HG_GROUP * HG_CHUNK)
    mid = HG_BLK * (t // HG_BLK) + HG_MID
    same_chunk = (t[None, :] // HG_CHUNK) == (t[:, None] // HG_CHUNK)
    lower = (t[None, :] <= t[:, None]) & same_chunk
    upper = (t[None, :] >= t[:, None]) & same_chunk
    fwd = np.concatenate([lower, lower[mid]], axis=0)
    bwd = np.concatenate([upper, upper[mid]], axis=0)
    return jnp.asarray(np.stack([fwd, bwd]).astype(np.float32), dtype=BF16)


def _split3_cols(x):
    hi = x.astype(BF16)
    r = x - hi.astype(F32)
    mid = r.astype(BF16)
    lo = (r - mid.astype(F32)).astype(BF16)
    return jnp.concatenate([hi, mid, lo], axis=1)


def _hgrn_prefix(q, fpre, v, lb, tri, reverse):
    rows, w = q.shape
    c = HG_CHUNK
    f = jnp.maximum(lb + (1.0 - lb) * jax.nn.sigmoid(fpre), F_MIN)
    lf = jnp.log(f)
    kk = 1.0 - f
    r = _dot(tri, _split3_cols(lf))
    cum = r[:, 2 * w:] + r[:, w:2 * w] + r[:, :w]
    b, bmid = cum[:rows], cum[rows:]
    blast = [b[g * c:g * c + 1] if reverse else b[(g + 1) * c - 1:(g + 1) * c] for g in range(rows // c)]
    blast_rows = jnp.concatenate([jnp.broadcast_to(x, (c, w)) for x in blast], axis=0)
    return dict(q=q, b=b, blast=blast, vb=v.astype(BF16), reverse=reverse,
                qe=(q * jnp.exp(b)).astype(BF16),
                kdec=(kk * jnp.exp(blast_rows - b)).astype(BF16),
                ktil=kk * jnp.exp(bmid - b))


def _hgrn_tilde(pre, rs, sl, blk_mask):
    qh, bh, reverse = pre["q"][rs, sl], pre["b"][rs, sl], pre["reverse"]
    c, w = qh.shape
    nblk = c // HG_BLK
    pieces = []
    for j in range(nblk):
        ref_row = bh[j * HG_BLK + HG_MID:j * HG_BLK + HG_MID + 1, :]
        if reverse:
            hi_row = (j + 1) * HG_BLK
            qt = qh[:hi_row] * jnp.exp(bh[:hi_row] - ref_row)
            if hi_row < c:
                qt = jnp.concatenate([qt, jnp.zeros((c - hi_row, w), F32)], axis=0)
        else:
            lo_row = j * HG_BLK
            qt = qh[lo_row:] * jnp.exp(bh[lo_row:] - ref_row)
            if lo_row > 0:
                qt = jnp.concatenate([jnp.zeros((lo_row, w), F32), qt], axis=0)
        pieces.append(qt)
    qtil = jnp.concatenate(pieces, axis=1).astype(BF16)
    kt = pre["ktil"][rs, sl]
    kt = jnp.concatenate([jnp.concatenate([kt, kt], axis=0)] * nblk, axis=1)
    return qtil, jnp.where(blk_mask, kt, 0.0).astype(BF16)


def _hgrn_kernel(qf_ref, ff_ref, vf_ref, qb_ref, fb_ref, vb_ref, lbf_ref, lbb_ref, tri_ref,
                 of_ref, ob_ref, sf_ref, sb_ref):
    @pl.when(pl.program_id(1) == 0)
    def _():
        sf_ref[...] = jnp.zeros_like(sf_ref)
        sb_ref[...] = jnp.zeros_like(sb_ref)

    c = HG_CHUNK
    ngrp = qf_ref.shape[0] // c
    nblk = c // HG_BLK
    npair = B_W // PAIR_W
    row = lax.broadcasted_iota(jnp.int32, (c, PAIR_W), 0)
    col = lax.broadcasted_iota(jnp.int32, (c, PAIR_W), 1) % HEAD_DIM
    r2 = lax.broadcasted_iota(jnp.int32, (PAIR_W, PAIR_W), 0) // HEAD_DIM
    c2 = lax.broadcasted_iota(jnp.int32, (PAIR_W, PAIR_W), 1) // HEAD_DIM
    same_head = r2 == c2
    krow = lax.broadcasted_iota(jnp.int32, (PAIR_W, nblk * PAIR_W), 0)
    klane = lax.broadcasted_iota(jnp.int32, (PAIR_W, nblk * PAIR_W), 1)
    blk_mask = ((krow // HEAD_DIM == (klane % PAIR_W) // HEAD_DIM)
                & ((krow % HEAD_DIM) // HG_BLK == klane // PAIR_W))

    pres = [_hgrn_prefix(qf_ref[...], ff_ref[...], vf_ref[...], lbf_ref[...], tri_ref[0], False),
            _hgrn_prefix(qb_ref[...], fb_ref[...], vb_ref[...], lbb_ref[...], tri_ref[1], True)]
    order = [list(range(ngrp)), list(range(ngrp - 1, -1, -1))]
    units = [(d, g, p) for d in range(2) for g in order[d] for p in range(npair)]
    sl = lambda p: slice(p * PAIR_W, (p + 1) * PAIR_W)
    rs = lambda g: slice(g * c, (g + 1) * c)
    state_refs = (sf_ref, sb_ref)

    upd = {}
    for d, g, p in units:
        full = _dot_tn(pres[d]["vb"][rs(g), sl(p)], pres[d]["kdec"][rs(g), sl(p)])
        upd[d, g, p] = jnp.where(same_head, full, 0.0)
    attn = {}
    for d, g, p in units:
        qtil, kt = _hgrn_tilde(pres[d], rs(g), sl(p), blk_mask)
        attn[d, g, p] = _dot_nt(qtil, kt)
    inter = {}
    for d in range(2):
        for p in range(npair):
            st = state_refs[d][p]
            for g in order[d]:
                inter[d, g, p] = _dot_nt(pres[d]["qe"][rs(g), sl(p)], st.astype(BF16))
                st = st * jnp.exp(pres[d]["blast"][g][:, sl(p)]) + upd[d, g, p]
            state_refs[d][p] = st
    outs = [[[None] * npair for _ in range(ngrp)] for _ in range(2)]
    for d, g, p in units:
        keep = (col >= row) if pres[d]["reverse"] else (col <= row)
        a = jnp.where(keep, attn[d, g, p], 0.0).astype(BF16)
        v2 = pres[d]["vb"][rs(g), sl(p)]
        vbd = jnp.where(same_head, jnp.concatenate([v2, v2], axis=0), 0.0)
        outs[d][g][p] = _dot(a, vbd) + inter[d, g, p]
    for d, o_ref in enumerate((of_ref, ob_ref)):
        o_ref[...] = jnp.concatenate([jnp.concatenate(per_pair, axis=1) for per_pair in outs[d]], axis=0)


def _hgrn(bmix, lbf, lbb, tri, seq):
    t = bmix.shape[0]
    bsz = t // seq
    c = HG_GROUP * HG_CHUNK
    n = seq // c
    fwd = lambda k: pl.BlockSpec((c, B_W), lambda b, j: (b * n + j, k))
    bwd = lambda k: pl.BlockSpec((c, B_W), lambda b, j: (b * n + n - 1 - j, k))
    const2 = lambda b, j: (0, 0)
    return pl.pallas_call(
        _hgrn_kernel,
        grid=(bsz, n),
        in_specs=[fwd(0), fwd(1), fwd(3), bwd(0), bwd(2), bwd(3),
                  pl.BlockSpec((1, B_W), const2), pl.BlockSpec((1, B_W), const2),
                  pl.BlockSpec(tri.shape, lambda b, j: (0, 0, 0))],
        out_specs=[fwd(0), bwd(0)],
        out_shape=[jax.ShapeDtypeStruct((t, B_W), F32), jax.ShapeDtypeStruct((t, B_W), F32)],
        scratch_shapes=[pltpu.VMEM((B_W // PAIR_W, PAIR_W, PAIR_W), F32)] * 2,
        compiler_params=_cparams("arbitrary", "arbitrary"),
        name="hgrn",
    )(bmix, bmix, bmix, bmix, bmix, bmix, lbf, lbb, tri)


def _out_proj_kernel(x_ref, oa_ref, of_ref, ob_ref, bg_ref, oc_ref, w_ref, gb_ref, bd_ref,
                     g1_ref, sh_ref, sc_ref, g_ref, xo_ref, h_ref):
    ob = of_ref[...] + ob_ref[...]
    ob = ob * lax.rsqrt(_head_mean_sq(ob, bd_ref[...]) + EPS) * gb_ref[...]
    ob = (ob * _silu(bg_ref[...])).astype(BF16)
    mix = (_dot(oa_ref[...], w_ref[:A_W, :]) + _dot(ob, w_ref[A_W:A_W + B_W, :])
           + _dot(oc_ref[...], w_ref[A_W + B_W:, :]))
    x = x_ref[...] + g1_ref[...] * mix
    xo_ref[...] = x
    y = x * lax.rsqrt(jnp.mean(x * x, axis=-1, keepdims=True) + EPS) * g_ref[...]
    h_ref[...] = (y * (1.0 + sc_ref[...]) + sh_ref[...]).astype(BF16)


def _out_proj(x2, oa, of, ob, bmix, oc, w, gb, bd, mod3, g, seq):
    t, d = x2.shape
    tm = TM_PROJ
    per_seq = seq // tm
    row = lambda i: (i, 0)
    const = lambda i: (0, 0)
    modspec = lambda k: pl.BlockSpec((None, 1, d), lambda i: (i // per_seq, 0, k))
    return pl.pallas_call(
        _out_proj_kernel,
        grid=(t // tm,),
        in_specs=[
            pl.BlockSpec((tm, d), row),
            pl.BlockSpec((tm, A_W), row),
            pl.BlockSpec((tm, B_W), row), pl.BlockSpec((tm, B_W), row),
            pl.BlockSpec((tm, B_W), lambda i: (i, 4)),
            pl.BlockSpec((tm, C_W), row),
            pl.BlockSpec(w.shape, const),
            pl.BlockSpec((1, B_W), const),
            pl.BlockSpec(bd.shape, const),
            modspec(2), modspec(3), modspec(4),
            pl.BlockSpec((1, d), const),
        ],
        out_specs=[pl.BlockSpec((tm, d), row), pl.BlockSpec((tm, d), row)],
        out_shape=[jax.ShapeDtypeStruct((t, d), F32), jax.ShapeDtypeStruct((t, d), BF16)],
        compiler_params=_cparams("arbitrary"),
        name="out_proj",
    )(x2, oa, of, ob, bmix, oc, w, gb, bd, mod3, mod3, mod3, g)


def _ffn_up_kernel(h_ref, wa_ref, wb_ref, cwa_ref, cwb_ref, cba_ref, cbb_ref, o_ref):
    h = h_ref[...]
    seq = h.shape[0]
    tn = o_ref.shape[1]
    row = lax.broadcasted_iota(jnp.int32, (seq, 1), 0)
    subs = [(off, min(MXU_W, tn - off)) for off in range(0, tn, MXU_W)]

    def matmuls(sub):
        off, w = sub
        if w == MXU_W:
            return _dot(h, wa_ref[:, off:off + w]), _dot(h, wb_ref[:, off:off + w])
        u = _dot(h, jnp.concatenate([wa_ref[:, off:off + w], wb_ref[:, off:off + w]], axis=1))
        return u[:, :w], u[:, w:]

    def conv(u, cw, cb):
        prev = jnp.where(row == 0, 0.0, pltpu.roll(u, 1, 0))
        nxt = jnp.where(row == seq - 1, 0.0, pltpu.roll(u, seq - 1, 0))
        return cw[0:1, :] * prev + cw[1:2, :] * u + cw[2:3, :] * nxt + cb

    def gate(sub, us):
        off, w = sub
        cols = slice(off, off + w)
        a = conv(us[0], cwa_ref[:, cols], cba_ref[:, cols])
        b = conv(us[1], cwb_ref[:, cols], cbb_ref[:, cols])
        o_ref[:, cols] = (_silu(a) * b).astype(BF16)

    _run_ahead(subs, matmuls, gate, ahead=2)


def _ffn_up(h, w_up, conv_w, conv_b, seq):
    t, d = h.shape
    dff = w_up.shape[1] // 2
    tn = TN_FFN
    nt = dff // tn
    lo = lambda b, j: (0, j)
    hi = lambda b, j: (0, nt + j)
    return pl.pallas_call(
        _ffn_up_kernel,
        grid=(t // seq, nt),
        in_specs=[
            pl.BlockSpec((seq, d), lambda b, j: (b, 0)),
            pl.BlockSpec((d, tn), lo), pl.BlockSpec((d, tn), hi),
            pl.BlockSpec((CONV_W, tn), lo), pl.BlockSpec((CONV_W, tn), hi),
            pl.BlockSpec((1, tn), lo), pl.BlockSpec((1, tn), hi),
        ],
        out_specs=pl.BlockSpec((seq, tn), lambda b, j: (b, j)),
        out_shape=jax.ShapeDtypeStruct((t, dff), BF16),
        compiler_params=_cparams("arbitrary", "arbitrary"),
        name="ffn_up",
    )(h, w_up, w_up, conv_w, conv_w, conv_b, conv_b)


def _ffn_down_kernel(x_ref, a_ref, w_ref, g2_ref, o_ref):
    o_ref[...] = x_ref[...] + g2_ref[...] * _dot(a_ref[...], w_ref[...])


def _ffn_down(x2, act, w, mod3, seq):
    t, d = x2.shape
    tm = TM_PROJ
    per_seq = seq // tm
    row = lambda i: (i, 0)
    return pl.pallas_call(
        _ffn_down_kernel,
        grid=(t // tm,),
        in_specs=[
            pl.BlockSpec((tm, d), row),
            pl.BlockSpec((tm, act.shape[1]), row),
            pl.BlockSpec(w.shape, lambda i: (0, 0)),
            pl.BlockSpec((None, 1, d), lambda i: (i // per_seq, 0, 5)),
        ],
        out_specs=pl.BlockSpec((tm, d), row),
        out_shape=jax.ShapeDtypeStruct((t, d), F32),
        compiler_params=_cparams("arbitrary"),
        name="ffn_down",
    )(x2, act, w, mod3)


def _rope_tables(seq):
    n_rows = seq // GRID_W
    rowp = np.repeat(np.arange(n_rows), GRID_W).astype(np.float32)
    colp = np.tile(np.arange(GRID_W), n_rows).astype(np.float32)
    half = HEAD_DIM // 2
    inv = (np.float32(ROPE_THETA) ** (-np.arange(0, half, 2, dtype=np.float32) / half)).astype(np.float32)
    ang_r = rowp[:, None] * inv
    ang_c = colp[:, None] * inv
    cos = np.concatenate([np.cos(ang_r), np.cos(ang_r), np.cos(ang_c), np.cos(ang_c)], axis=1)
    sin = np.concatenate([-np.sin(ang_r), np.sin(ang_r), -np.sin(ang_c), np.sin(ang_c)], axis=1)
    tile = lambda a: jnp.asarray(np.tile(a.astype(np.float32), (1, A_HEADS)))
    return tile(cos), tile(sin)


def _head_block_ones(width):
    i = np.arange(width) // HEAD_DIM
    return jnp.asarray((i[:, None] == i[None, :]).astype(np.float32), dtype=BF16)


def kernel(x, c, w_ada, b_ada, norm_g, w_in, a_q_norm, a_k_norm, b_lb, b_out_norm, c_q_norm, c_k_norm,
           w_out, w_up, conv_w, conv_b, w_down):
    bsz, seq, d = x.shape
    depth = w_in.shape[0]
    t = bsz * seq
    scale = HEAD_DIM ** -0.5

    mod = _ada(c, w_ada, b_ada)
    lb_all = _hgrn_lb(b_lb.astype(F32))
    cos, sin = _rope_tables(seq)
    bd = _head_block_ones(A_W)
    tri = _hgrn_tri_constants()
    bias_strided, bias_local = _dilated_bias_tables(seq)
    tile = lambda gvec, heads, s=1.0: (jnp.tile(gvec.astype(F32), heads) * s).reshape(1, -1)

    x2 = x.reshape(t, d)
    for l in range(depth):
        mod3 = mod[l].reshape(bsz, 1, 6 * d)
        aqt, ak, avt, bmix, cq, ck, cv = _in_proj(
            x2, mod3, norm_g[l, 0].reshape(1, d), w_in[l].astype(BF16), cos, sin,
            tile(a_q_norm[l], A_HEADS, scale), tile(a_k_norm[l], A_KV_HEADS),
            tile(c_q_norm[l], C_HEADS, scale), tile(c_k_norm[l], C_KV_HEADS), bd, seq)
        o_a = _attention(aqt, ak, avt, seq, n_heads=A_HEADS, n_kv=A_KV_HEADS)
        far = _dil_strided(cq, ck, cv, bias_strided, seq)
        o_c = _dil_local(cq, ck, cv, bias_local, far, seq)
        o_f, o_b = _hgrn(bmix, lb_all[0, l].reshape(1, B_W), lb_all[1, l].reshape(1, B_W), tri, seq)
        x2, h2 = _out_proj(x2, o_a, o_f, o_b, bmix, o_c, w_out[l].astype(BF16),
                           tile(b_out_norm[l], B_HEADS), bd[:B_W, :B_W], mod3,
                           norm_g[l, 1].reshape(1, d), seq)
        act = _ffn_up(h2, w_up[l].astype(BF16), conv_w[l], conv_b[l].reshape(1, -1), seq)
        x2 = _ffn_down(x2, act, w_down[l].astype(BF16), mod3, seq)
    return x2.reshape(bsz, seq, d)
```

```python
import functools

import numpy as np
import jax
import jax.numpy as jnp
from jax import lax
from jax.experimental import pallas as pl
from jax.experimental.pallas import tpu as pltpu

F32 = jnp.float32
BF16 = jnp.bfloat16

HEAD_DIM = 64
GRID_W = 64
EPS = 1e-6
NEG_BIG = -1e30
F_MIN = 1e-6
ROPE_THETA = 10000.0
A_HEADS, A_KV_HEADS = 6, 2
B_HEADS = 4
C_HEADS, C_KV_HEADS = 6, 2
C_BRANCHES = ((128, 1), (512, 4), (2048, 16))
CONV_W = 3

A_W = A_HEADS * HEAD_DIM
KV_W = A_KV_HEADS * HEAD_DIM
B_W = B_HEADS * HEAD_DIM
C_W = C_HEADS * HEAD_DIM
OFF_AQ, OFF_AK, OFF_AV = 0, A_W, A_W + KV_W
OFF_B = A_W + 2 * KV_W
OFF_CQ = OFF_B + 5 * B_W
OFF_CK, OFF_CV = OFF_CQ + C_W, OFF_CQ + C_W + KV_W
IN_CUT = OFF_CQ - KV_W
assert OFF_AV % 256 == 0 and IN_CUT % 256 == 0 and (OFF_CV + KV_W) % 256 == 0

TM_PROJ = 512
TQ_ATTN = 256
KEY_CHUNK = 256
SCORE_ROWS = 1024
ATTN_AHEAD = 3
HG_CHUNK = 64
HG_GROUP = 4
HG_BLK = 8
HG_MID = 4
MXU_W = 256
TN_FFN = 1408
TN_ADA = 1536
VMEM_LIMIT = 56 * 1024 * 1024


def _cparams(*sem):
    return pltpu.CompilerParams(dimension_semantics=sem, vmem_limit_bytes=VMEM_LIMIT)


def _dot(a, b):
    return jnp.dot(a, b, preferred_element_type=F32)


def _dot_nt(a, b):
    return lax.dot_general(a, b, (((1,), (1,)), ((), ())), preferred_element_type=F32)


def _dot_tn(a, b):
    return lax.dot_general(a, b, (((0,), (0,)), ((), ())), preferred_element_type=F32)


def _split2(x):
    hi = x.astype(BF16)
    lo = (x - hi.astype(F32)).astype(BF16)
    return hi, lo


def _head_mean_sq(x, bd):
    hi, lo = _split2(x * x)
    return (_dot(hi, bd) + _dot(lo, bd)) * (1.0 / HEAD_DIM)


def _silu(x):
    return x * jax.nn.sigmoid(x)


def _cast_weights_once(w_ref, wb_ref):
    @pl.when(pl.program_id(0) == 0)
    def _():
        wb_ref[...] = w_ref[...].astype(BF16)


def _layer_weight_spec(w_all, layer):
    return pl.BlockSpec((None,) + w_all.shape[1:], lambda i: (layer, 0, 0))


def _ada_kernel(c_ref, w_ref, b_ref, o_ref):
    a = _silu(c_ref[...]).astype(BF16)
    o_ref[...] = _dot(a, w_ref[...].astype(BF16)) + b_ref[...]


def _ada(c, w_ada, b_ada):
    depth, d, n = w_ada.shape
    bsz = c.shape[0]
    return pl.pallas_call(
        _ada_kernel,
        grid=(depth, n // TN_ADA),
        in_specs=[
            pl.BlockSpec((bsz, d), lambda l, j: (0, 0)),
            pl.BlockSpec((None, d, TN_ADA), lambda l, j: (l, 0, j)),
            pl.BlockSpec((None, 1, TN_ADA), lambda l, j: (l, 0, j)),
        ],
        out_specs=pl.BlockSpec((None, bsz, TN_ADA), lambda l, j: (l, 0, j)),
        out_shape=jax.ShapeDtypeStruct((depth, bsz, n), F32),
        compiler_params=_cparams("arbitrary", "arbitrary"),
        name="ada",
    )(c, w_ada, b_ada.reshape(depth, 1, n))


def _lb_kernel(b_ref, o_ref):
    depth = b_ref.shape[1]
    for d in range(2):
        rows = [b_ref[d, l:l + 1, :] for l in range(depth)]
        m = functools.reduce(jnp.maximum, rows)
        e = [jnp.exp(r - m) for r in rows]
        tot = functools.reduce(lambda a, b: a + b, e)
        run = jnp.zeros_like(m)
        for l in range(depth):
            sm = e[l] / tot
            run = run + sm
            o_ref[d, l:l + 1, :] = run - e[0] / tot


def _hgrn_lb(b_lb):
    return pl.pallas_call(
        _lb_kernel,
        out_shape=jax.ShapeDtypeStruct(b_lb.shape, F32),
        name="hgrn_lb",
    )(b_lb)


def _rope(z, cos, sin):
    n = z.shape[-1]
    lane = lax.broadcasted_iota(jnp.int32, z.shape, 1)
    up = pltpu.roll(z, n - 16, 1)
    dn = pltpu.roll(z, 16, 1)
    return z * cos + jnp.where((lane % 32) < 16, up, dn) * sin


def _in_proj_kernel(x_ref, sh_ref, sc_ref, g_ref, w_ref, cos_ref, sin_ref,
                    gaq_ref, gak_ref, gcq_ref, gck_ref, bd_ref,
                    aqt_ref, ak_ref, avt_ref, bmix_ref, cq_ref, ck_ref, cv_ref, wb_ref):
    _cast_weights_once(w_ref, wb_ref)

    def headnorm(z, gain):
        n = z.shape[-1]
        return z * lax.rsqrt(_head_mean_sq(z, bd_ref[:n, :n]) + EPS) * gain

    x = x_ref[...]
    y = x * lax.rsqrt(jnp.mean(x * x, axis=-1, keepdims=True) + EPS) * g_ref[...]
    h = (y * (1.0 + sc_ref[...]) + sh_ref[...]).astype(BF16)

    cuts = (0, OFF_AV, IN_CUT, OFF_CV + KV_W)
    proj = lambda i: _dot(h, wb_ref[:, cuts[i]:cuts[i + 1]])
    cos, sin = cos_ref[...], sin_ref[...]
    d0 = proj(0)
    d1 = proj(1)
    aqt_ref[...] = _rope(headnorm(d0[:, :A_W], gaq_ref[...]), cos, sin).T.astype(BF16)
    ak_ref[...] = _rope(headnorm(d0[:, A_W:], gak_ref[...]), cos[:, :KV_W], sin[:, :KV_W]).astype(BF16)
    d2 = proj(2)
    avt_ref[...] = d1[:, :KV_W].T.astype(BF16)
    bmix_ref[:, :IN_CUT - OFF_B] = d1[:, KV_W:]
    bmix_ref[:, IN_CUT - OFF_B:] = d2[:, :OFF_CQ - IN_CUT]
    c0 = OFF_CQ - IN_CUT
    cq_ref[...] = headnorm(d2[:, c0:c0 + C_W], gcq_ref[...])
    ck_ref[...] = headnorm(d2[:, c0 + C_W:c0 + C_W + KV_W], gck_ref[...])
    cv_ref[...] = d2[:, c0 + C_W + KV_W:]


def _in_proj(x2, mod3, g, w_all, layer, cos, sin, gaq, gak, gcq, gck, bd, seq):
    t, d = x2.shape
    tm = TM_PROJ
    per_seq = seq // tm
    row = lambda i: (i, 0)
    const = lambda i: (0, 0)
    modspec = lambda k: pl.BlockSpec((None, 1, d), lambda i: (i // per_seq, 0, k))
    return pl.pallas_call(
        _in_proj_kernel,
        grid=(t // tm,),
        in_specs=[
            pl.BlockSpec((tm, d), row),
            modspec(0), modspec(1),
            pl.BlockSpec((1, d), const),
            _layer_weight_spec(w_all, layer),
            pl.BlockSpec((tm, A_W), lambda i: (i % per_seq, 0)),
            pl.BlockSpec((tm, A_W), lambda i: (i % per_seq, 0)),
            pl.BlockSpec((1, A_W), const), pl.BlockSpec((1, KV_W), const),
            pl.BlockSpec((1, C_W), const), pl.BlockSpec((1, KV_W), const),
            pl.BlockSpec(bd.shape, const),
        ],
        out_specs=[
            pl.BlockSpec((A_W, tm), lambda i: (0, i)),
            pl.BlockSpec((tm, KV_W), row),
            pl.BlockSpec((KV_W, tm), lambda i: (0, i)),
            pl.BlockSpec((tm, 5 * B_W), row),
            pl.BlockSpec((tm, C_W), row),
            pl.BlockSpec((tm, KV_W), row),
            pl.BlockSpec((tm, KV_W), row),
        ],
        out_shape=[
            jax.ShapeDtypeStruct((A_W, t), BF16),
            jax.ShapeDtypeStruct((t, KV_W), BF16),
            jax.ShapeDtypeStruct((KV_W, t), BF16),
            jax.ShapeDtypeStruct((t, 5 * B_W), F32),
            jax.ShapeDtypeStruct((t, C_W), F32),
            jax.ShapeDtypeStruct((t, KV_W), F32),
            jax.ShapeDtypeStruct((t, KV_W), F32),
        ],
        scratch_shapes=[pltpu.VMEM(w_all.shape[1:], BF16)],
        compiler_params=_cparams("arbitrary"),
        name="in_proj",
    )(x2, mod3, mod3, g, w_all, cos, sin, gaq, gak, gcq, gck, bd)


def _alibi_slopes():
    return [float(s) for s in (2.0 ** (-8.0 * np.arange(1, C_HEADS + 1) / C_HEADS)).astype(np.float32)]


def _run_ahead(items, start, finish, ahead=ATTN_AHEAD):
    pending = {}
    for k in range(len(items) + ahead):
        if k < len(items):
            pending[k] = start(items[k])
        if k >= ahead:
            finish(items[k - ahead], pending.pop(k - ahead))


def _attn_kernel(qt_ref, k_ref, vt_ref, o_ref, s_ref, *, n_heads, n_kv):
    group = n_heads // n_kv
    tq, kc = TQ_ATTN, KEY_CHUNK
    seq = k_ref.shape[0]
    zeros = jnp.zeros((HEAD_DIM, tq), BF16)
    fold = lambda x, op: op(x.reshape(kc // 8, 8, tq), axis=0)
    items = [(i, h) for i in range(qt_ref.shape[1] // tq) for h in range(n_heads)]
    outs = []

    def scores(item):
        i, h = item
        qt = qt_ref[h * HEAD_DIM:(h + 1) * HEAD_DIM, i * tq:(i + 1) * tq]
        qtz = jnp.concatenate([qt, zeros] if h // group == 0 else [zeros, qt], axis=0)
        slot = items.index(item) % s_ref.shape[0]
        for r in range(0, seq, SCORE_ROWS):
            s_ref[slot, r:r + SCORE_ROWS, :] = _dot(k_ref[r:r + SCORE_ROWS, :], qtz)
        return slot

    def finish(item, slot):
        i, h = item
        j = h // group
        m8 = fold(s_ref[slot, 0:kc, :], jnp.max)
        for r in range(kc, seq, kc):
            m8 = jnp.maximum(m8, fold(s_ref[slot, r:r + kc, :], jnp.max))
        m = jnp.max(m8, axis=0, keepdims=True)
        l8 = jnp.zeros((8, tq), F32)
        ot = jnp.zeros((HEAD_DIM, tq), F32)
        for r in range(0, seq, kc):
            p = jnp.exp(s_ref[slot, r:r + kc, :] - m)
            l8 = l8 + fold(p, jnp.sum)
            ot = ot + _dot(vt_ref[j * HEAD_DIM:(j + 1) * HEAD_DIM, r:r + kc], p.astype(BF16))
        outs.append(ot / jnp.sum(l8, axis=0, keepdims=True))
        if h == n_heads - 1:
            o_ref[i * tq:(i + 1) * tq, :] = jnp.concatenate(outs[-n_heads:], axis=0).T.astype(o_ref.dtype)

    _run_ahead(items, scores, finish)


def _attention(qt, k, vt, seq, *, n_heads, n_kv):
    qw, t = qt.shape
    assert n_kv == 2 and k.shape[1] == n_kv * HEAD_DIM
    return pl.pallas_call(
        functools.partial(_attn_kernel, n_heads=n_heads, n_kv=n_kv),
        grid=(t // seq,),
        in_specs=[
            pl.BlockSpec((qw, seq), lambda b: (0, b)),
            pl.BlockSpec((seq, k.shape[1]), lambda b: (b, 0)),
            pl.BlockSpec((vt.shape[0], seq), lambda b: (0, b)),
        ],
        out_specs=pl.BlockSpec((seq, qw), lambda b: (b, 0)),
        out_shape=jax.ShapeDtypeStruct((t, qw), BF16),
        scratch_shapes=[pltpu.VMEM((ATTN_AHEAD + 1, seq, TQ_ATTN), F32)],
        compiler_params=_cparams("arbitrary"),
        name="attn_rope",
    )(qt, k, vt)


DIL_RES = 4
DIL_TQ = 128
DIL_WIN = 3 * DIL_TQ
PAIR_W = 2 * HEAD_DIM
assert C_BRANCHES == ((128, 1), (512, 4), (2048, 16)) and C_BRANCHES[0][0] // 2 <= DIL_TQ


def _dilated_bias_tables(seq):
    slopes = jnp.asarray(_alibi_slopes(), F32).reshape(C_HEADS, 1, 1)
    n = seq // DIL_RES
    i = jnp.arange(n, dtype=jnp.int32)
    a = jnp.abs(i[None, :] - i[:, None])
    w2, w3 = C_BRANCHES[1][0] // (2 * DIL_RES), C_BRANCHES[2][0] // (2 * DIL_RES)
    step3 = C_BRANCHES[2][1] // DIL_RES
    count = (a <= w2).astype(jnp.int32) + ((a <= w3) & (a % step3 == 0)).astype(jnp.int32)
    log_mult = jnp.where(count == 2, float(np.log(2.0)), jnp.where(count == 1, 0.0, NEG_BIG))
    strided = log_mult[None] - slopes * (DIL_RES * a).astype(F32)[None]
    r = jnp.arange(DIL_TQ, dtype=jnp.int32)[:, None]
    c = jnp.arange(DIL_WIN, dtype=jnp.int32)[None, :]
    half = C_BRANCHES[0][0] // 2
    local = []
    for shift in (0, DIL_TQ, 2 * DIL_TQ):
        d = jnp.abs(c - shift - r)
        local.append(jnp.where((d <= half)[None], -slopes * d.astype(F32)[None], NEG_BIG))
    return strided, jnp.stack(local)


def _kv_variants(kt, v_f32):
    zeros = jnp.zeros((HEAD_DIM, kt.shape[1]), BF16)
    ktz = [[jnp.concatenate([kt[j * HEAD_DIM:(j + 1) * HEAD_DIM], zeros], axis=0) for j in range(2)],
           [jnp.concatenate([zeros, kt[j * HEAD_DIM:(j + 1) * HEAD_DIM]], axis=0) for j in range(2)]]
    v = v_f32.astype(BF16)
    vswap = pltpu.roll(v_f32, HEAD_DIM, 1).astype(BF16)
    vsel = [[v, vswap], [vswap, v]]
    return ktz, vsel


def _dil_strided_kernel(q0_ref, q1_ref, q2_ref, k_ref, v_ref, bias_ref,
                        o0_ref, o1_ref, o2_ref, l0_ref, l1_ref, l2_ref):
    n = k_ref.shape[0] // DIL_RES
    rows = pl.ds(pl.program_id(1), n, stride=DIL_RES)
    group = C_HEADS // C_KV_HEADS
    ktz, vsel = _kv_variants(k_ref[rows, :].T.astype(BF16), v_ref[rows, :])
    left = lax.broadcasted_iota(jnp.int32, (n, PAIR_W), 1) < HEAD_DIM
    slabs = ((q0_ref, o0_ref, l0_ref), (q1_ref, o1_ref, l1_ref), (q2_ref, o2_ref, l2_ref))
    qs = [q_ref[rows, :].astype(BF16) for q_ref, _, _ in slabs]
    res, lse = [], []

    def scores(h):
        return _dot(qs[h // 2], ktz[h % 2][h // group]) + bias_ref[h]

    def finish(h, s):
        m = jnp.max(s, axis=-1, keepdims=True)
        p = jnp.exp(s - m)
        l = jnp.sum(p, axis=-1, keepdims=True)
        res.append(_dot(p.astype(BF16), vsel[h % 2][h // group]) / l)
        lse.append(m + jnp.log(l))
        if h % 2 == 1:
            _, o_ref, l_ref = slabs[h // 2]
            o_ref[rows, :] = jnp.where(left, res[-2], res[-1])
            l_ref[rows, :] = jnp.where(left, lse[-2], lse[-1])

    _run_ahead(list(range(C_HEADS)), scores, finish)


def _dil_strided(cq, ck, cv, bias, seq):
    t = cq.shape[0]
    bsz = t // seq
    blk = lambda col: pl.BlockSpec((seq, PAIR_W), lambda b, r: (b, col))
    slab = jax.ShapeDtypeStruct((t, PAIR_W), F32)
    return pl.pallas_call(
        _dil_strided_kernel,
        grid=(bsz, DIL_RES),
        in_specs=[blk(0), blk(1), blk(2), blk(0), blk(0),
                  pl.BlockSpec(bias.shape, lambda b, r: (0, 0, 0))],
        out_specs=[blk(0)] * 6,
        out_shape=[slab] * 6,
        compiler_params=_cparams("arbitrary", "arbitrary"),
        name="attn_dil_strided",
    )(cq, cq, cq, ck, cv, bias)


def _dil_local_kernel(q0_ref, q1_ref, q2_ref, k_ref, v_ref, bias_ref,
                      f0_ref, f1_ref, f2_ref, l0_ref, l1_ref, l2_ref, o_ref):
    seq = k_ref.shape[0]
    tq, win = DIL_TQ, DIL_WIN
    group = C_HEADS // C_KV_HEADS
    ktz, vsel = _kv_variants(k_ref[...].T.astype(BF16), v_ref[...])
    left = lax.broadcasted_iota(jnp.int32, (tq, PAIR_W), 1) < HEAD_DIM
    slabs = ((q0_ref, f0_ref, l0_ref), (q1_ref, f1_ref, l1_ref), (q2_ref, f2_ref, l2_ref))
    items = [(i, slab, half) for i in range(seq // tq) for slab in range(3) for half in range(2)]

    def window(i):
        t0 = i * tq
        w0 = min(max(t0 - tq, 0), seq - win)
        return t0, w0, (t0 - w0) // tq

    def scores(item):
        i, slab, half = item
        t0, w0, variant = window(i)
        h = 2 * slab + half
        q = slabs[slab][0][t0:t0 + tq, :].astype(BF16)
        return _dot(q, ktz[half][h // group][:, w0:w0 + win]) + bias_ref[variant, h]

    def finish(item, s, acc):
        i, slab, half = item
        t0, w0, _ = window(i)
        m = jnp.max(s, axis=-1, keepdims=True)
        p = jnp.exp(s - m)
        near = _dot(p.astype(BF16), vsel[half][(2 * slab + half) // group][w0:w0 + win, :])
        acc.append((near, m, jnp.sum(p, axis=-1, keepdims=True)))
        if half == 0:
            return
        (n0, m0, s0), (n1, m1, s1) = acc[-2], acc[-1]
        near, m_near, l_near = jnp.where(left, n0, n1), jnp.where(left, m0, m1), jnp.where(left, s0, s1)
        _, far_ref, lse_ref = slabs[slab]
        lse_far = lse_ref[t0:t0 + tq, :]
        top = jnp.maximum(m_near, lse_far)
        w_near = jnp.exp(m_near - top)
        w_far = jnp.exp(lse_far - top)
        merged = (near * w_near + far_ref[t0:t0 + tq, :] * w_far) / (l_near * w_near + w_far)
        o_ref[t0:t0 + tq, slab * PAIR_W:(slab + 1) * PAIR_W] = merged.astype(o_ref.dtype)

    acc = []
    _run_ahead(items, scores, lambda item, s: finish(item, s, acc))


def _dil_local(cq, ck, cv, bias, far, seq):
    t = cq.shape[0]
    blk = lambda col: pl.BlockSpec((seq, PAIR_W), lambda b: (b, col))
    return pl.pallas_call(
        _dil_local_kernel,
        grid=(t // seq,),
        in_specs=[blk(0), blk(1), blk(2), blk(0), blk(0),
                  pl.BlockSpec(bias.shape, lambda b: (0, 0, 0, 0))] + [blk(0)] * 6,
        out_specs=pl.BlockSpec((seq, C_W), lambda b: (b, 0)),
        out_shape=jax.ShapeDtypeStruct((t, C_W), BF16),
        compiler_params=_cparams("arbitrary"),
        name="attn_dil_local",
    )(cq, cq, cq, ck, cv, bias, *far)


def _hgrn_tri_constants():
    t = np.arange(HG_GROUP * HG_CHUNK)
    mid = HG_BLK * (t // HG_BLK) + HG_MID
    same_chunk = (t[None, :] // HG_CHUNK) == (t[:, None] // HG_CHUNK)
    lower = (t[None, :] <= t[:, None]) & same_chunk
    upper = (t[None, :] >= t[:, None]) & same_chunk
    fwd = np.concatenate([lower, lower[mid]], axis=0)
    bwd = np.concatenate([upper, upper[mid]], axis=0)
    return jnp.asarray(np.stack([fwd, bwd]).astype(np.float32), dtype=BF16)


def _split3_cols(x):
    hi = x.astype(BF16)
    r = x - hi.astype(F32)
    mid = r.astype(BF16)
    lo = (r - mid.astype(F32)).astype(BF16)
    return jnp.concatenate([hi, mid, lo], axis=1)


def _hgrn_prefix(q, fpre, v, lb, tri, reverse):
    rows, w = q.shape
    c = HG_CHUNK
    f = jnp.maximum(lb + (1.0 - lb) * jax.nn.sigmoid(fpre), F_MIN)
    lf = jnp.log(f)
    kk = 1.0 - f
    r = _dot(tri, _split3_cols(lf))
    cum = r[:, 2 * w:] + r[:, w:2 * w] + r[:, :w]
    b, bmid = cum[:rows], cum[rows:]
    blast = [b[g * c:g * c + 1] if reverse else b[(g + 1) * c - 1:(g + 1) * c] for g in range(rows // c)]
    blast_rows = jnp.concatenate([jnp.broadcast_to(x, (c, w)) for x in blast], axis=0)
    return dict(q=q, b=b, blast=blast, vb=v.astype(BF16), reverse=reverse,
                qe=(q * jnp.exp(b)).astype(BF16),
                kdec=(kk * jnp.exp(blast_rows - b)).astype(BF16),
                ktil=kk * jnp.exp(bmid - b))


def _hgrn_tilde(pre, rs, sl, blk_mask):
    qh, bh, reverse = pre["q"][rs, sl], pre["b"][rs, sl], pre["reverse"]
    c = qh.shape[0]
    nblk = c // HG_BLK
    pieces = []
    for j in range(nblk):
        ref_row = bh[j * HG_BLK + HG_MID:j * HG_BLK + HG_MID + 1, :]
        if reverse:
            hi_row = (j + 1) * HG_BLK
            qt = qh[:hi_row] * jnp.exp(bh[:hi_row] - ref_row)
            if hi_row < c:
                qt = jnp.concatenate([qt, jnp.zeros((c - hi_row, HEAD_DIM), F32)], axis=0)
        else:
            lo_row = j * HG_BLK
            qt = qh[lo_row:] * jnp.exp(bh[lo_row:] - ref_row)
            if lo_row > 0:
                qt = jnp.concatenate([jnp.zeros((lo_row, HEAD_DIM), F32), qt], axis=0)
        pieces.append(qt)
    qtil = jnp.concatenate(pieces, axis=1).astype(BF16)
    kt = jnp.concatenate([pre["ktil"][rs, sl]] * nblk, axis=1)
    return qtil, jnp.where(blk_mask, kt, 0.0).astype(BF16)


def _hgrn_kernel(qf_ref, ff_ref, vf_ref, qb_ref, fb_ref, vb_ref, lbf_ref, lbb_ref, tri_ref,
                 of_ref, ob_ref, sf_ref, sb_ref):
    @pl.when(pl.program_id(1) == 0)
    def _():
        sf_ref[...] = jnp.zeros_like(sf_ref)
        sb_ref[...] = jnp.zeros_like(sb_ref)

    c = HG_CHUNK
    ngrp = qf_ref.shape[0] // c
    nblk = c // HG_BLK
    row = lax.broadcasted_iota(jnp.int32, (c, c), 0)
    col = lax.broadcasted_iota(jnp.int32, (c, c), 1)
    brow = lax.broadcasted_iota(jnp.int32, (c, nblk * HEAD_DIM), 0) // HG_BLK
    bcol = lax.broadcasted_iota(jnp.int32, (c, nblk * HEAD_DIM), 1) // HEAD_DIM
    blk_mask = brow == bcol

    pres = [_hgrn_prefix(qf_ref[...], ff_ref[...], vf_ref[...], lbf_ref[...], tri_ref[0], False),
            _hgrn_prefix(qb_ref[...], fb_ref[...], vb_ref[...], lbb_ref[...], tri_ref[1], True)]
    order = [list(range(ngrp)), list(range(ngrp - 1, -1, -1))]
    units = [(d, g, h) for d in range(2) for g in order[d] for h in range(B_HEADS)]
    sl = lambda h: slice(h * HEAD_DIM, (h + 1) * HEAD_DIM)
    rs = lambda g: slice(g * c, (g + 1) * c)
    state_refs = (sf_ref, sb_ref)

    upd = {u: _dot_tn(pres[u[0]]["vb"][rs(u[1]), sl(u[2])], pres[u[0]]["kdec"][rs(u[1]), sl(u[2])])
           for u in units}
    attn = {}
    for d, g, h in units:
        qtil, kt = _hgrn_tilde(pres[d], rs(g), sl(h), blk_mask)
        attn[d, g, h] = _dot_nt(qtil, kt)
    inter = {}
    for d in range(2):
        for h in range(B_HEADS):
            st = state_refs[d][h]
            for g in order[d]:
                inter[d, g, h] = _dot_nt(pres[d]["qe"][rs(g), sl(h)], st.astype(BF16))
                st = st * jnp.exp(pres[d]["blast"][g][:, sl(h)]) + upd[d, g, h]
            state_refs[d][h] = st
    outs = [[[None] * B_HEADS for _ in range(ngrp)] for _ in range(2)]
    for d, g, h in units:
        keep = (col >= row) if pres[d]["reverse"] else (col <= row)
        a = jnp.where(keep, attn[d, g, h], 0.0).astype(BF16)
        outs[d][g][h] = _dot(a, pres[d]["vb"][rs(g), sl(h)]) + inter[d, g, h]
    for d, o_ref in enumerate((of_ref, ob_ref)):
        o_ref[...] = jnp.concatenate([jnp.concatenate(per_head, axis=1) for per_head in outs[d]], axis=0)


def _hgrn(bmix, lbf, lbb, tri, seq):
    t = bmix.shape[0]
    bsz = t // seq
    c = HG_GROUP * HG_CHUNK
    n = seq // c
    fwd = lambda k: pl.BlockSpec((c, B_W), lambda b, j: (b * n + j, k))
    bwd = lambda k: pl.BlockSpec((c, B_W), lambda b, j: (b * n + n - 1 - j, k))
    const2 = lambda b, j: (0, 0)
    return pl.pallas_call(
        _hgrn_kernel,
        grid=(bsz, n),
        in_specs=[fwd(0), fwd(1), fwd(3), bwd(0), bwd(2), bwd(3),
                  pl.BlockSpec((1, B_W), const2), pl.BlockSpec((1, B_W), const2),
                  pl.BlockSpec(tri.shape, lambda b, j: (0, 0, 0))],
        out_specs=[fwd(0), bwd(0)],
        out_shape=[jax.ShapeDtypeStruct((t, B_W), F32), jax.ShapeDtypeStruct((t, B_W), F32)],
        scratch_shapes=[pltpu.VMEM((B_HEADS, HEAD_DIM, HEAD_DIM), F32)] * 2,
        compiler_params=_cparams("arbitrary", "arbitrary"),
        name="hgrn",
    )(bmix, bmix, bmix, bmix, bmix, bmix, lbf, lbb, tri)


def _out_proj_kernel(x_ref, oa_ref, of_ref, ob_ref, bg_ref, oc_ref, w_ref, gb_ref, bd_ref,
                     g1_ref, sh_ref, sc_ref, g_ref, xo_ref, h_ref, wb_ref):
    _cast_weights_once(w_ref, wb_ref)
    ob = of_ref[...] + ob_ref[...]
    ob = ob * lax.rsqrt(_head_mean_sq(ob, bd_ref[...]) + EPS) * gb_ref[...]
    ob = (ob * _silu(bg_ref[...])).astype(BF16)
    mix = (_dot(oa_ref[...], wb_ref[:A_W, :]) + _dot(ob, wb_ref[A_W:A_W + B_W, :])
           + _dot(oc_ref[...], wb_ref[A_W + B_W:, :]))
    x = x_ref[...] + g1_ref[...] * mix
    xo_ref[...] = x
    y = x * lax.rsqrt(jnp.mean(x * x, axis=-1, keepdims=True) + EPS) * g_ref[...]
    h_ref[...] = (y * (1.0 + sc_ref[...]) + sh_ref[...]).astype(BF16)


def _out_proj(x2, oa, of, ob, bmix, oc, w_all, layer, gb, bd, mod3, g, seq):
    t, d = x2.shape
    tm = TM_PROJ
    per_seq = seq // tm
    row = lambda i: (i, 0)
    const = lambda i: (0, 0)
    modspec = lambda k: pl.BlockSpec((None, 1, d), lambda i: (i // per_seq, 0, k))
    return pl.pallas_call(
        _out_proj_kernel,
        grid=(t // tm,),
        in_specs=[
            pl.BlockSpec((tm, d), row),
            pl.BlockSpec((tm, A_W), row),
            pl.BlockSpec((tm, B_W), row), pl.BlockSpec((tm, B_W), row),
            pl.BlockSpec((tm, B_W), lambda i: (i, 4)),
            pl.BlockSpec((tm, C_W), row),
            _layer_weight_spec(w_all, layer),
            pl.BlockSpec((1, B_W), const),
            pl.BlockSpec(bd.shape, const),
            modspec(2), modspec(3), modspec(4),
            pl.BlockSpec((1, d), const),
        ],
        out_specs=[pl.BlockSpec((tm, d), row), pl.BlockSpec((tm, d), row)],
        out_shape=[jax.ShapeDtypeStruct((t, d), F32), jax.ShapeDtypeStruct((t, d), BF16)],
        scratch_shapes=[pltpu.VMEM(w_all.shape[1:], BF16)],
        compiler_params=_cparams("arbitrary"),
        name="out_proj",
    )(x2, oa, of, ob, bmix, oc, w_all, gb, bd, mod3, mod3, mod3, g)


def _ffn_up_kernel(h_ref, wa_ref, wb_ref, cwa_ref, cwb_ref, cba_ref, cbb_ref, o_ref):
    h = h_ref[...]
    seq = h.shape[0]
    tn = o_ref.shape[1]
    row = lax.broadcasted_iota(jnp.int32, (seq, 1), 0)
    subs = [(off, min(MXU_W, tn - off)) for off in range(0, tn, MXU_W)]

    def matmuls(sub):
        off, w = sub
        if w == MXU_W:
            return _dot(h, wa_ref[:, off:off + w]), _dot(h, wb_ref[:, off:off + w])
        u = _dot(h, jnp.concatenate([wa_ref[:, off:off + w], wb_ref[:, off:off + w]], axis=1))
        return u[:, :w], u[:, w:]

    def conv(u, cw, cb):
        prev = jnp.where(row == 0, 0.0, pltpu.roll(u, 1, 0))
        nxt = jnp.where(row == seq - 1, 0.0, pltpu.roll(u, seq - 1, 0))
        return cw[0:1, :] * prev + cw[1:2, :] * u + cw[2:3, :] * nxt + cb

    def gate(sub, us):
        off, w = sub
        cols = slice(off, off + w)
        a = conv(us[0], cwa_ref[:, cols], cba_ref[:, cols])
        b = conv(us[1], cwb_ref[:, cols], cbb_ref[:, cols])
        o_ref[:, cols] = (_silu(a) * b).astype(BF16)

    _run_ahead(subs, matmuls, gate, ahead=2)


def _ffn_up(h, w_up, conv_w, conv_b, seq):
    t, d = h.shape
    dff = w_up.shape[1] // 2
    tn = TN_FFN
    nt = dff // tn
    lo = lambda b, j: (0, j)
    hi = lambda b, j: (0, nt + j)
    return pl.pallas_call(
        _ffn_up_kernel,
        grid=(t // seq, nt),
        in_specs=[
            pl.BlockSpec((seq, d), lambda b, j: (b, 0)),
            pl.BlockSpec((d, tn), lo), pl.BlockSpec((d, tn), hi),
            pl.BlockSpec((CONV_W, tn), lo), pl.BlockSpec((CONV_W, tn), hi),
            pl.BlockSpec((1, tn), lo), pl.BlockSpec((1, tn), hi),
        ],
        out_specs=pl.BlockSpec((seq, tn), lambda b, j: (b, j)),
        out_shape=jax.ShapeDtypeStruct((t, dff), BF16),
        compiler_params=_cparams("arbitrary", "arbitrary"),
        name="ffn_up",
    )(h, w_up, w_up, conv_w, conv_w, conv_b, conv_b)


def _ffn_down_kernel(x_ref, a_ref, w_ref, g2_ref, o_ref, wb_ref):
    _cast_weights_once(w_ref, wb_ref)
    o_ref[...] = x_ref[...] + g2_ref[...] * _dot(a_ref[...], wb_ref[...])


def _ffn_down(x2, act, w_all, layer, mod3, seq):
    t, d = x2.shape
    tm = TM_PROJ
    per_seq = seq // tm
    row = lambda i: (i, 0)
    return pl.pallas_call(
        _ffn_down_kernel,
        grid=(t // tm,),
        in_specs=[
            pl.BlockSpec((tm, d), row),
            pl.BlockSpec((tm, act.shape[1]), row),
            _layer_weight_spec(w_all, layer),
            pl.BlockSpec((None, 1, d), lambda i: (i // per_seq, 0, 5)),
        ],
        out_specs=pl.BlockSpec((tm, d), row),
        out_shape=jax.ShapeDtypeStruct((t, d), F32),
        scratch_shapes=[pltpu.VMEM(w_all.shape[1:], BF16)],
        compiler_params=_cparams("arbitrary"),
        name="ffn_down",
    )(x2, act, w_all, mod3)


def _rope_tables(seq):
    n_rows = seq // GRID_W
    rowp = np.repeat(np.arange(n_rows), GRID_W).astype(np.float32)
    colp = np.tile(np.arange(GRID_W), n_rows).astype(np.float32)
    half = HEAD_DIM // 2
    inv = (np.float32(ROPE_THETA) ** (-np.arange(0, half, 2, dtype=np.float32) / half)).astype(np.float32)
    ang_r = rowp[:, None] * inv
    ang_c = colp[:, None] * inv
    cos = np.concatenate([np.cos(ang_r), np.cos(ang_r), np.cos(ang_c), np.cos(ang_c)], axis=1)
    sin = np.concatenate([-np.sin(ang_r), np.sin(ang_r), -np.sin(ang_c), np.sin(ang_c)], axis=1)
    tile = lambda a: jnp.asarray(np.tile(a.astype(np.float32), (1, A_HEADS)))
    return tile(cos), tile(sin)


def _head_block_ones(width):
    i = np.arange(width) // HEAD_DIM
    return jnp.asarray((i[:, None] == i[None, :]).astype(np.float32), dtype=BF16)


def kernel(x, c, w_ada, b_ada, norm_g, w_in, a_q_norm, a_k_norm, b_lb, b_out_norm, c_q_norm, c_k_norm,
           w_out, w_up, conv_w, conv_b, w_down):
    bsz, seq, d = x.shape
    depth = w_in.shape[0]
    t = bsz * seq
    scale = HEAD_DIM ** -0.5

    mod = _ada(c, w_ada, b_ada)
    lb_all = _hgrn_lb(b_lb.astype(F32))
    cos, sin = _rope_tables(seq)
    bd = _head_block_ones(A_W)
    tri = _hgrn_tri_constants()
    bias_strided, bias_local = _dilated_bias_tables(seq)
    tile = lambda gvec, heads, s=1.0: (jnp.tile(gvec.astype(F32), heads) * s).reshape(1, -1)

    x2 = x.reshape(t, d)
    for l in range(depth):
        mod3 = mod[l].reshape(bsz, 1, 6 * d)
        aqt, ak, avt, bmix, cq, ck, cv = _in_proj(
            x2, mod3, norm_g[l, 0].reshape(1, d), w_in, l, cos, sin,
            tile(a_q_norm[l], A_HEADS, scale), tile(a_k_norm[l], A_KV_HEADS),
            tile(c_q_norm[l], C_HEADS, scale), tile(c_k_norm[l], C_KV_HEADS), bd, seq)
        o_a = _attention(aqt, ak, avt, seq, n_heads=A_HEADS, n_kv=A_KV_HEADS)
        far = _dil_strided(cq, ck, cv, bias_strided, seq)
        o_c = _dil_local(cq, ck, cv, bias_local, far, seq)
        o_f, o_b = _hgrn(bmix, lb_all[0, l].reshape(1, B_W), lb_all[1, l].reshape(1, B_W), tri, seq)
        x2, h2 = _out_proj(x2, o_a, o_f, o_b, bmix, o_c, w_out, l,
                           tile(b_out_norm[l], B_HEADS), bd[:B_W, :B_W], mod3,
                           norm_g[l, 1].reshape(1, d), seq)
        act = _ffn_up(h2, w_up[l].astype(BF16), conv_w[l], conv_b[l].reshape(1, -1), seq)
        x2 = _ffn_down(x2, act, w_down, l, mod3, seq)
    return x2.reshape(bsz, seq, d)
```

```python
import functools

import numpy as np
import jax
import jax.numpy as jnp
from jax import lax
from jax.experimental import pallas as pl
from jax.experimental.pallas import tpu as pltpu

F32 = jnp.float32
BF16 = jnp.bfloat16

HEAD_DIM = 64
GRID_W = 64
EPS = 1e-6
NEG_BIG = -1e30
F_MIN = 1e-6
ROPE_THETA = 10000.0
A_HEADS, A_KV_HEADS = 6, 2
B_HEADS = 4
C_HEADS, C_KV_HEADS = 6, 2
C_BRANCHES = ((128, 1), (512, 4), (2048, 16))
CONV_W = 3

A_W = A_HEADS * HEAD_DIM
KV_W = A_KV_HEADS * HEAD_DIM
B_W = B_HEADS * HEAD_DIM
C_W = C_HEADS * HEAD_DIM
OFF_AQ, OFF_AK, OFF_AV = 0, A_W, A_W + KV_W
OFF_B = A_W + 2 * KV_W
OFF_CQ = OFF_B + 5 * B_W
OFF_CK, OFF_CV = OFF_CQ + C_W, OFF_CQ + C_W + KV_W
IN_CUT = OFF_CQ - KV_W
assert OFF_AV % 256 == 0 and IN_CUT % 256 == 0 and (OFF_CV + KV_W) % 256 == 0

TM_PROJ = 512
TQ_ATTN = 256
KEY_CHUNK = 256
SCORE_ROWS = 256
ATTN_AHEAD = 3
HG_CHUNK = 64
HG_GROUP = 4
HG_BLK = 8
HG_MID = 4
MXU_W = 256
TN_FFN = 1408
TN_ADA = 1536
VMEM_LIMIT = 56 * 1024 * 1024


def _cparams(*sem):
    return pltpu.CompilerParams(dimension_semantics=sem, vmem_limit_bytes=VMEM_LIMIT)


def _dot(a, b):
    return jnp.dot(a, b, preferred_element_type=F32)


def _dot_nt(a, b):
    return lax.dot_general(a, b, (((1,), (1,)), ((), ())), preferred_element_type=F32)


def _dot_tn(a, b):
    return lax.dot_general(a, b, (((0,), (0,)), ((), ())), preferred_element_type=F32)


def _split2(x):
    hi = x.astype(BF16)
    lo = (x - hi.astype(F32)).astype(BF16)
    return hi, lo


def _head_mean_sq(x, bd):
    hi, lo = _split2(x * x)
    return (_dot(hi, bd) + _dot(lo, bd)) * (1.0 / HEAD_DIM)


def _silu(x):
    return x * jax.nn.sigmoid(x)


def _cast_weights_once(w_ref, wb_ref):
    @pl.when(pl.program_id(0) == 0)
    def _():
        wb_ref[...] = w_ref[...].astype(BF16)


def _layer_weight_spec(w_all, layer):
    return pl.BlockSpec((None,) + w_all.shape[1:], lambda i: (layer, 0, 0))


def _ada_kernel(c_ref, w_ref, b_ref, o_ref):
    a = _silu(c_ref[...]).astype(BF16)
    o_ref[...] = _dot(a, w_ref[...].astype(BF16)) + b_ref[...]


def _ada(c, w_ada, b_ada):
    depth, d, n = w_ada.shape
    bsz = c.shape[0]
    return pl.pallas_call(
        _ada_kernel,
        grid=(depth, n // TN_ADA),
        in_specs=[
            pl.BlockSpec((bsz, d), lambda l, j: (0, 0)),
            pl.BlockSpec((None, d, TN_ADA), lambda l, j: (l, 0, j)),
            pl.BlockSpec((None, 1, TN_ADA), lambda l, j: (l, 0, j)),
        ],
        out_specs=pl.BlockSpec((None, bsz, TN_ADA), lambda l, j: (l, 0, j)),
        out_shape=jax.ShapeDtypeStruct((depth, bsz, n), F32),
        compiler_params=_cparams("arbitrary", "arbitrary"),
        name="ada",
    )(c, w_ada, b_ada.reshape(depth, 1, n))


def _lb_kernel(b_ref, o_ref):
    depth = b_ref.shape[1]
    for d in range(2):
        rows = [b_ref[d, l:l + 1, :] for l in range(depth)]
        m = functools.reduce(jnp.maximum, rows)
        e = [jnp.exp(r - m) for r in rows]
        tot = functools.reduce(lambda a, b: a + b, e)
        run = jnp.zeros_like(m)
        for l in range(depth):
            sm = e[l] / tot
            run = run + sm
            o_ref[d, l:l + 1, :] = run - e[0] / tot


def _hgrn_lb(b_lb):
    return pl.pallas_call(
        _lb_kernel,
        out_shape=jax.ShapeDtypeStruct(b_lb.shape, F32),
        name="hgrn_lb",
    )(b_lb)


def _rope(z, cos, sin):
    n = z.shape[-1]
    lane = lax.broadcasted_iota(jnp.int32, z.shape, 1)
    up = pltpu.roll(z, n - 16, 1)
    dn = pltpu.roll(z, 16, 1)
    return z * cos + jnp.where((lane % 32) < 16, up, dn) * sin


def _in_proj_kernel(x_ref, sh_ref, sc_ref, g_ref, w_ref, cos_ref, sin_ref,
                    gaq_ref, gak_ref, gcq_ref, gck_ref, bd_ref,
                    aqt_ref, ak_ref, avt_ref, bmix_ref, cq_ref, ck_ref, cv_ref, wb_ref):
    _cast_weights_once(w_ref, wb_ref)

    def headnorm(z, gain):
        n = z.shape[-1]
        return z * lax.rsqrt(_head_mean_sq(z, bd_ref[:n, :n]) + EPS) * gain

    x = x_ref[...]
    y = x * lax.rsqrt(jnp.mean(x * x, axis=-1, keepdims=True) + EPS) * g_ref[...]
    h = (y * (1.0 + sc_ref[...]) + sh_ref[...]).astype(BF16)

    cuts = (0, OFF_AV, IN_CUT, OFF_CV + KV_W)
    proj = lambda i: _dot(h, wb_ref[:, cuts[i]:cuts[i + 1]])
    cos, sin = cos_ref[...], sin_ref[...]
    d0 = proj(0)
    d1 = proj(1)
    aqt_ref[...] = _rope(headnorm(d0[:, :A_W], gaq_ref[...]), cos, sin).T.astype(BF16)
    ak_ref[...] = _rope(headnorm(d0[:, A_W:], gak_ref[...]), cos[:, :KV_W], sin[:, :KV_W]).astype(BF16)
    d2 = proj(2)
    avt_ref[...] = d1[:, :KV_W].T.astype(BF16)
    bmix_ref[:, :IN_CUT - OFF_B] = d1[:, KV_W:]
    bmix_ref[:, IN_CUT - OFF_B:] = d2[:, :OFF_CQ - IN_CUT]
    c0 = OFF_CQ - IN_CUT
    cq_ref[...] = headnorm(d2[:, c0:c0 + C_W], gcq_ref[...])
    ck_ref[...] = headnorm(d2[:, c0 + C_W:c0 + C_W + KV_W], gck_ref[...])
    cv_ref[...] = d2[:, c0 + C_W + KV_W:]


def _in_proj(x2, mod3, g, w_all, layer, cos, sin, gaq, gak, gcq, gck, bd, seq):
    t, d = x2.shape
    tm = TM_PROJ
    per_seq = seq // tm
    row = lambda i: (i, 0)
    const = lambda i: (0, 0)
    modspec = lambda k: pl.BlockSpec((None, 1, d), lambda i: (i // per_seq, 0, k))
    return pl.pallas_call(
        _in_proj_kernel,
        grid=(t // tm,),
        in_specs=[
            pl.BlockSpec((tm, d), row),
            modspec(0), modspec(1),
            pl.BlockSpec((1, d), const),
            _layer_weight_spec(w_all, layer),
            pl.BlockSpec((tm, A_W), lambda i: (i % per_seq, 0)),
            pl.BlockSpec((tm, A_W), lambda i: (i % per_seq, 0)),
            pl.BlockSpec((1, A_W), const), pl.BlockSpec((1, KV_W), const),
            pl.BlockSpec((1, C_W), const), pl.BlockSpec((1, KV_W), const),
            pl.BlockSpec(bd.shape, const),
        ],
        out_specs=[
            pl.BlockSpec((A_W, tm), lambda i: (0, i)),
            pl.BlockSpec((tm, KV_W), row),
            pl.BlockSpec((KV_W, tm), lambda i: (0, i)),
            pl.BlockSpec((tm, 5 * B_W), row),
            pl.BlockSpec((tm, C_W), row),
            pl.BlockSpec((tm, KV_W), row),
            pl.BlockSpec((tm, KV_W), row),
        ],
        out_shape=[
            jax.ShapeDtypeStruct((A_W, t), BF16),
            jax.ShapeDtypeStruct((t, KV_W), BF16),
            jax.ShapeDtypeStruct((KV_W, t), BF16),
            jax.ShapeDtypeStruct((t, 5 * B_W), F32),
            jax.ShapeDtypeStruct((t, C_W), F32),
            jax.ShapeDtypeStruct((t, KV_W), F32),
            jax.ShapeDtypeStruct((t, KV_W), F32),
        ],
        scratch_shapes=[pltpu.VMEM(w_all.shape[1:], BF16)],
        compiler_params=_cparams("arbitrary"),
        name="in_proj",
    )(x2, mod3, mod3, g, w_all, cos, sin, gaq, gak, gcq, gck, bd)


def _alibi_slopes():
    return [float(s) for s in (2.0 ** (-8.0 * np.arange(1, C_HEADS + 1) / C_HEADS)).astype(np.float32)]


def _run_ahead(items, start, finish, ahead=ATTN_AHEAD):
    pending = {}
    for k in range(len(items) + ahead):
        if k < len(items):
            pending[k] = start(items[k])
        if k >= ahead:
            finish(items[k - ahead], pending.pop(k - ahead))


def _attn_kernel(qt_ref, k_ref, vt_ref, o_ref, s_ref, *, n_heads, n_kv):
    group = n_heads // n_kv
    tq, kc = TQ_ATTN, KEY_CHUNK
    seq = k_ref.shape[0]
    zeros = jnp.zeros((HEAD_DIM, tq), BF16)
    fold = lambda x, op: op(x.reshape(kc // 8, 8, tq), axis=0)
    items = [(i, h) for i in range(qt_ref.shape[1] // tq) for h in range(n_heads)]
    outs = []

    def scores(item):
        i, h = item
        qt = qt_ref[h * HEAD_DIM:(h + 1) * HEAD_DIM, i * tq:(i + 1) * tq]
        qtz = jnp.concatenate([qt, zeros] if h // group == 0 else [zeros, qt], axis=0)
        slot = items.index(item) % s_ref.shape[0]
        for r in range(0, seq, SCORE_ROWS):
            s_ref[slot, r:r + SCORE_ROWS, :] = _dot(k_ref[r:r + SCORE_ROWS, :], qtz)
        return slot

    def finish(item, slot):
        i, h = item
        j = h // group
        m8 = fold(s_ref[slot, 0:kc, :], jnp.max)
        for r in range(kc, seq, kc):
            m8 = jnp.maximum(m8, fold(s_ref[slot, r:r + kc, :], jnp.max))
        m = jnp.max(m8, axis=0, keepdims=True)
        l8 = jnp.zeros((8, tq), F32)
        ot = jnp.zeros((HEAD_DIM, tq), F32)
        for r in range(0, seq, kc):
            p = jnp.exp(s_ref[slot, r:r + kc, :] - m)
            l8 = l8 + fold(p, jnp.sum)
            ot = ot + _dot(vt_ref[j * HEAD_DIM:(j + 1) * HEAD_DIM, r:r + kc], p.astype(BF16))
        outs.append(ot / jnp.sum(l8, axis=0, keepdims=True))
        if h == n_heads - 1:
            o_ref[i * tq:(i + 1) * tq, :] = jnp.concatenate(outs[-n_heads:], axis=0).T.astype(o_ref.dtype)

    _run_ahead(items, scores, finish)


def _attention(qt, k, vt, seq, *, n_heads, n_kv):
    qw, t = qt.shape
    assert n_kv == 2 and k.shape[1] == n_kv * HEAD_DIM
    return pl.pallas_call(
        functools.partial(_attn_kernel, n_heads=n_heads, n_kv=n_kv),
        grid=(t // seq,),
        in_specs=[
            pl.BlockSpec((qw, seq), lambda b: (0, b)),
            pl.BlockSpec((seq, k.shape[1]), lambda b: (b, 0)),
            pl.BlockSpec((vt.shape[0], seq), lambda b: (0, b)),
        ],
        out_specs=pl.BlockSpec((seq, qw), lambda b: (b, 0)),
        out_shape=jax.ShapeDtypeStruct((t, qw), BF16),
        scratch_shapes=[pltpu.VMEM((ATTN_AHEAD + 1, seq, TQ_ATTN), F32)],
        compiler_params=_cparams("arbitrary"),
        name="attn_rope",
    )(qt, k, vt)


DIL_RES = 4
DIL_TQ = 128
DIL_WIN = 3 * DIL_TQ
PAIR_W = 2 * HEAD_DIM
assert C_BRANCHES == ((128, 1), (512, 4), (2048, 16)) and C_BRANCHES[0][0] // 2 <= DIL_TQ


def _dilated_bias_tables(seq):
    slopes = jnp.asarray(_alibi_slopes(), F32).reshape(C_HEADS, 1, 1)
    n = seq // DIL_RES
    i = jnp.arange(n, dtype=jnp.int32)
    a = jnp.abs(i[None, :] - i[:, None])
    w2, w3 = C_BRANCHES[1][0] // (2 * DIL_RES), C_BRANCHES[2][0] // (2 * DIL_RES)
    step3 = C_BRANCHES[2][1] // DIL_RES
    count = (a <= w2).astype(jnp.int32) + ((a <= w3) & (a % step3 == 0)).astype(jnp.int32)
    log_mult = jnp.where(count == 2, float(np.log(2.0)), jnp.where(count == 1, 0.0, NEG_BIG))
    strided = log_mult[None] - slopes * (DIL_RES * a).astype(F32)[None]
    r = jnp.arange(DIL_TQ, dtype=jnp.int32)[:, None]
    c = jnp.arange(DIL_WIN, dtype=jnp.int32)[None, :]
    half = C_BRANCHES[0][0] // 2
    local = []
    for shift in (0, DIL_TQ, 2 * DIL_TQ):
        d = jnp.abs(c - shift - r)
        local.append(jnp.where((d <= half)[None], -slopes * d.astype(F32)[None], NEG_BIG))
    return strided, jnp.stack(local)


def _kv_variants(kt, v_f32):
    zeros = jnp.zeros((HEAD_DIM, kt.shape[1]), BF16)
    ktz = [[jnp.concatenate([kt[j * HEAD_DIM:(j + 1) * HEAD_DIM], zeros], axis=0) for j in range(2)],
           [jnp.concatenate([zeros, kt[j * HEAD_DIM:(j + 1) * HEAD_DIM]], axis=0) for j in range(2)]]
    v = v_f32.astype(BF16)
    vswap = pltpu.roll(v_f32, HEAD_DIM, 1).astype(BF16)
    vsel = [[v, vswap], [vswap, v]]
    return ktz, vsel


def _dil_strided_kernel(q0_ref, q1_ref, q2_ref, k_ref, v_ref, bias_ref,
                        o0_ref, o1_ref, o2_ref, l0_ref, l1_ref, l2_ref):
    n = k_ref.shape[0] // DIL_RES
    group = C_HEADS // C_KV_HEADS
    left = lax.broadcasted_iota(jnp.int32, (n, PAIR_W), 1) < HEAD_DIM
    slabs = ((q0_ref, o0_ref, l0_ref), (q1_ref, o1_ref, l1_ref), (q2_ref, o2_ref, l2_ref))
    rows = [pl.ds(r, n, stride=DIL_RES) for r in range(DIL_RES)]
    kv = [_kv_variants(k_ref[rows[r], :].T.astype(BF16), v_ref[rows[r], :]) for r in range(DIL_RES)]
    items = [(r, h) for r in range(DIL_RES) for h in range(C_HEADS)]
    res, lse = [], []

    def scores(item):
        r, h = item
        q = slabs[h // 2][0][rows[r], :].astype(BF16)
        return _dot(q, kv[r][0][h % 2][h // group]) + bias_ref[h]

    def finish(item, s):
        r, h = item
        m = jnp.max(s, axis=-1, keepdims=True)
        p = jnp.exp(s - m)
        l = jnp.sum(p, axis=-1, keepdims=True)
        res.append(_dot(p.astype(BF16), kv[r][1][h % 2][h // group]) / l)
        lse.append(m + jnp.log(l))
        if h % 2 == 1:
            _, o_ref, l_ref = slabs[h // 2]
            o_ref[rows[r], :] = jnp.where(left, res[-2], res[-1])
            l_ref[rows[r], :] = jnp.where(left, lse[-2], lse[-1])

    _run_ahead(items, scores, finish)


def _dil_strided(cq, ck, cv, bias, seq):
    t = cq.shape[0]
    blk = lambda col: pl.BlockSpec((seq, PAIR_W), lambda b: (b, col))
    slab = jax.ShapeDtypeStruct((t, PAIR_W), F32)
    return pl.pallas_call(
        _dil_strided_kernel,
        grid=(t // seq,),
        in_specs=[blk(0), blk(1), blk(2), blk(0), blk(0),
                  pl.BlockSpec(bias.shape, lambda b: (0, 0, 0))],
        out_specs=[blk(0)] * 6,
        out_shape=[slab] * 6,
        compiler_params=_cparams("arbitrary"),
        name="attn_dil_strided",
    )(cq, cq, cq, ck, cv, bias)


def _dil_local_kernel(q0_ref, q1_ref, q2_ref, k_ref, v_ref, bias_ref,
                      f0_ref, f1_ref, f2_ref, l0_ref, l1_ref, l2_ref, o_ref):
    seq = k_ref.shape[0]
    tq, win = DIL_TQ, DIL_WIN
    group = C_HEADS // C_KV_HEADS
    ktz, vsel = _kv_variants(k_ref[...].T.astype(BF16), v_ref[...])
    left = lax.broadcasted_iota(jnp.int32, (tq, PAIR_W), 1) < HEAD_DIM
    slabs = ((q0_ref, f0_ref, l0_ref), (q1_ref, f1_ref, l1_ref), (q2_ref, f2_ref, l2_ref))
    items = [(i, slab, half) for i in range(seq // tq) for slab in range(3) for half in range(2)]

    def window(i):
        t0 = i * tq
        w0 = min(max(t0 - tq, 0), seq - win)
        return t0, w0, (t0 - w0) // tq

    def scores(item):
        i, slab, half = item
        t0, w0, variant = window(i)
        h = 2 * slab + half
        q = slabs[slab][0][t0:t0 + tq, :].astype(BF16)
        return _dot(q, ktz[half][h // group][:, w0:w0 + win]) + bias_ref[variant, h]

    def finish(item, s, acc):
        i, slab, half = item
        t0, w0, _ = window(i)
        m = jnp.max(s, axis=-1, keepdims=True)
        p = jnp.exp(s - m)
        near = _dot(p.astype(BF16), vsel[half][(2 * slab + half) // group][w0:w0 + win, :])
        acc.append((near, m, jnp.sum(p, axis=-1, keepdims=True)))
        if half == 0:
            return
        (n0, m0, s0), (n1, m1, s1) = acc[-2], acc[-1]
        near, m_near, l_near = jnp.where(left, n0, n1), jnp.where(left, m0, m1), jnp.where(left, s0, s1)
        _, far_ref, lse_ref = slabs[slab]
        lse_far = lse_ref[t0:t0 + tq, :]
        top = jnp.maximum(m_near, lse_far)
        w_near = jnp.exp(m_near - top)
        w_far = jnp.exp(lse_far - top)
        merged = (near * w_near + far_ref[t0:t0 + tq, :] * w_far) / (l_near * w_near + w_far)
        o_ref[t0:t0 + tq, slab * PAIR_W:(slab + 1) * PAIR_W] = merged.astype(o_ref.dtype)

    acc = []
    _run_ahead(items, scores, lambda item, s: finish(item, s, acc))


def _dil_local(cq, ck, cv, bias, far, seq):
    t = cq.shape[0]
    blk = lambda col: pl.BlockSpec((seq, PAIR_W), lambda b: (b, col))
    return pl.pallas_call(
        _dil_local_kernel,
        grid=(t // seq,),
        in_specs=[blk(0), blk(1), blk(2), blk(0), blk(0),
                  pl.BlockSpec(bias.shape, lambda b: (0, 0, 0, 0))] + [blk(0)] * 6,
        out_specs=pl.BlockSpec((seq, C_W), lambda b: (b, 0)),
        out_shape=jax.ShapeDtypeStruct((t, C_W), BF16),
        compiler_params=_cparams("arbitrary"),
        name="attn_dil_local",
    )(cq, cq, cq, ck, cv, bias, *far)


def _hgrn_tri_constants():
    t = np.arange(HG_GROUP * HG_CHUNK)
    same_chunk = (t[None, :] // HG_CHUNK) == (t[:, None] // HG_CHUNK)
    lower = (t[None, :] <= t[:, None]) & same_chunk
    upper = (t[None, :] >= t[:, None]) & same_chunk
    return jnp.asarray(np.stack([lower, upper]).astype(np.float32), dtype=BF16)


def _split3_cols(x):
    hi = x.astype(BF16)
    r = x - hi.astype(F32)
    mid = r.astype(BF16)
    lo = (r - mid.astype(F32)).astype(BF16)
    return jnp.concatenate([hi, mid, lo], axis=1)


def _hgrn_prefix(q, fpre, v, lb, tri, reverse):
    rows, w = q.shape
    c = HG_CHUNK
    f = jnp.maximum(lb + (1.0 - lb) * jax.nn.sigmoid(fpre), F_MIN)
    lf = jnp.log(f)
    kk = 1.0 - f
    r = _dot(tri, _split3_cols(lf))
    b = r[:, 2 * w:] + r[:, w:2 * w] + r[:, :w]
    b3 = b.reshape(rows // HG_BLK, HG_BLK, w)
    bmid = jnp.broadcast_to(b3[:, HG_MID:HG_MID + 1, :], b3.shape).reshape(rows, w)
    blast = [b[g * c:g * c + 1] if reverse else b[(g + 1) * c - 1:(g + 1) * c] for g in range(rows // c)]
    blast_rows = jnp.concatenate([jnp.broadcast_to(x, (c, w)) for x in blast], axis=0)
    return dict(q=q, b=b, blast=blast, vb=v.astype(BF16), reverse=reverse,
                qe=(q * jnp.exp(b)).astype(BF16),
                kdec=(kk * jnp.exp(blast_rows - b)).astype(BF16),
                ktil=kk * jnp.exp(bmid - b))


def _hgrn_tilde(pre, rs, sl, blk_mask):
    qh, bh, reverse = pre["q"][rs, sl], pre["b"][rs, sl], pre["reverse"]
    c = qh.shape[0]
    nblk = c // HG_BLK
    pieces = []
    for j in range(nblk):
        ref_row = bh[j * HG_BLK + HG_MID:j * HG_BLK + HG_MID + 1, :]
        if reverse:
            hi_row = (j + 1) * HG_BLK
            qt = qh[:hi_row] * jnp.exp(bh[:hi_row] - ref_row)
            if hi_row < c:
                qt = jnp.concatenate([qt, jnp.zeros((c - hi_row, HEAD_DIM), F32)], axis=0)
        else:
            lo_row = j * HG_BLK
            qt = qh[lo_row:] * jnp.exp(bh[lo_row:] - ref_row)
            if lo_row > 0:
                qt = jnp.concatenate([jnp.zeros((lo_row, HEAD_DIM), F32), qt], axis=0)
        pieces.append(qt)
    qtil = jnp.concatenate(pieces, axis=1).astype(BF16)
    kt = jnp.concatenate([pre["ktil"][rs, sl]] * nblk, axis=1)
    return qtil, jnp.where(blk_mask, kt, 0.0).astype(BF16)


def _hgrn_kernel(qf_ref, ff_ref, vf_ref, qb_ref, fb_ref, vb_ref, lbf_ref, lbb_ref, tri_ref,
                 of_ref, ob_ref, sf_ref, sb_ref):
    @pl.when(pl.program_id(1) == 0)
    def _():
        sf_ref[...] = jnp.zeros_like(sf_ref)
        sb_ref[...] = jnp.zeros_like(sb_ref)

    c = HG_CHUNK
    ngrp = qf_ref.shape[0] // c
    nblk = c // HG_BLK
    row = lax.broadcasted_iota(jnp.int32, (c, c), 0)
    col = lax.broadcasted_iota(jnp.int32, (c, c), 1)
    brow = lax.broadcasted_iota(jnp.int32, (c, nblk * HEAD_DIM), 0) // HG_BLK
    bcol = lax.broadcasted_iota(jnp.int32, (c, nblk * HEAD_DIM), 1) // HEAD_DIM
    blk_mask = brow == bcol

    pres = [_hgrn_prefix(qf_ref[...], ff_ref[...], vf_ref[...], lbf_ref[...], tri_ref[0], False),
            _hgrn_prefix(qb_ref[...], fb_ref[...], vb_ref[...], lbb_ref[...], tri_ref[1], True)]
    order = [list(range(ngrp)), list(range(ngrp - 1, -1, -1))]
    units = [(d, g, h) for d in range(2) for g in order[d] for h in range(B_HEADS)]
    sl = lambda h: slice(h * HEAD_DIM, (h + 1) * HEAD_DIM)
    rs = lambda g: slice(g * c, (g + 1) * c)
    state_refs = (sf_ref, sb_ref)

    upd = {u: _dot_tn(pres[u[0]]["vb"][rs(u[1]), sl(u[2])], pres[u[0]]["kdec"][rs(u[1]), sl(u[2])])
           for u in units}
    attn = {}
    for d, g, h in units:
        qtil, kt = _hgrn_tilde(pres[d], rs(g), sl(h), blk_mask)
        attn[d, g, h] = _dot_nt(qtil, kt)
    inter = {}
    for d in range(2):
        for h in range(B_HEADS):
            st = state_refs[d][h]
            for g in order[d]:
                inter[d, g, h] = _dot_nt(pres[d]["qe"][rs(g), sl(h)], st.astype(BF16))
                st = st * jnp.exp(pres[d]["blast"][g][:, sl(h)]) + upd[d, g, h]
            state_refs[d][h] = st
    outs = [[[None] * B_HEADS for _ in range(ngrp)] for _ in range(2)]
    for d, g, h in units:
        keep = (col >= row) if pres[d]["reverse"] else (col <= row)
        a = jnp.where(keep, attn[d, g, h], 0.0).astype(BF16)
        outs[d][g][h] = _dot(a, pres[d]["vb"][rs(g), sl(h)]) + inter[d, g, h]
    for d, o_ref in enumerate((of_ref, ob_ref)):
        o_ref[...] = jnp.concatenate([jnp.concatenate(per_head, axis=1) for per_head in outs[d]], axis=0)


def _hgrn(bmix, lbf, lbb, tri, seq):
    t = bmix.shape[0]
    bsz = t // seq
    c = HG_GROUP * HG_CHUNK
    n = seq // c
    fwd = lambda k: pl.BlockSpec((c, B_W), lambda b, j: (b * n + j, k))
    bwd = lambda k: pl.BlockSpec((c, B_W), lambda b, j: (b * n + n - 1 - j, k))
    const2 = lambda b, j: (0, 0)
    return pl.pallas_call(
        _hgrn_kernel,
        grid=(bsz, n),
        in_specs=[fwd(0), fwd(1), fwd(3), bwd(0), bwd(2), bwd(3),
                  pl.BlockSpec((1, B_W), const2), pl.BlockSpec((1, B_W), const2),
                  pl.BlockSpec(tri.shape, lambda b, j: (0, 0, 0))],
        out_specs=[fwd(0), bwd(0)],
        out_shape=[jax.ShapeDtypeStruct((t, B_W), F32), jax.ShapeDtypeStruct((t, B_W), F32)],
        scratch_shapes=[pltpu.VMEM((B_HEADS, HEAD_DIM, HEAD_DIM), F32)] * 2,
        compiler_params=_cparams("arbitrary", "arbitrary"),
        name="hgrn",
    )(bmix, bmix, bmix, bmix, bmix, bmix, lbf, lbb, tri)


def _out_proj_kernel(x_ref, oa_ref, of_ref, ob_ref, bg_ref, oc_ref, w_ref, gb_ref, bd_ref,
                     g1_ref, sh_ref, sc_ref, g_ref, xo_ref, h_ref, wb_ref):
    _cast_weights_once(w_ref, wb_ref)
    ob = of_ref[...] + ob_ref[...]
    ob = ob * lax.rsqrt(_head_mean_sq(ob, bd_ref[...]) + EPS) * gb_ref[...]
    ob = (ob * _silu(bg_ref[...])).astype(BF16)
    mix = (_dot(oa_ref[...], wb_ref[:A_W, :]) + _dot(ob, wb_ref[A_W:A_W + B_W, :])
           + _dot(oc_ref[...], wb_ref[A_W + B_W:, :]))
    x = x_ref[...] + g1_ref[...] * mix
    xo_ref[...] = x
    y = x * lax.rsqrt(jnp.mean(x * x, axis=-1, keepdims=True) + EPS) * g_ref[...]
    h_ref[...] = (y * (1.0 + sc_ref[...]) + sh_ref[...]).astype(BF16)


def _out_proj(x2, oa, of, ob, bmix, oc, w_all, layer, gb, bd, mod3, g, seq):
    t, d = x2.shape
    tm = TM_PROJ
    per_seq = seq // tm
    row = lambda i: (i, 0)
    const = lambda i: (0, 0)
    modspec = lambda k: pl.BlockSpec((None, 1, d), lambda i: (i // per_seq, 0, k))
    return pl.pallas_call(
        _out_proj_kernel,
        grid=(t // tm,),
        in_specs=[
            pl.BlockSpec((tm, d), row),
            pl.BlockSpec((tm, A_W), row),
            pl.BlockSpec((tm, B_W), row), pl.BlockSpec((tm, B_W), row),
            pl.BlockSpec((tm, B_W), lambda i: (i, 4)),
            pl.BlockSpec((tm, C_W), row),
            _layer_weight_spec(w_all, layer),
            pl.BlockSpec((1, B_W), const),
            pl.BlockSpec(bd.shape, const),
            modspec(2), modspec(3), modspec(4),
            pl.BlockSpec((1, d), const),
        ],
        out_specs=[pl.BlockSpec((tm, d), row), pl.BlockSpec((tm, d), row)],
        out_shape=[jax.ShapeDtypeStruct((t, d), F32), jax.ShapeDtypeStruct((t, d), BF16)],
        scratch_shapes=[pltpu.VMEM(w_all.shape[1:], BF16)],
        compiler_params=_cparams("arbitrary"),
        name="out_proj",
    )(x2, oa, of, ob, bmix, oc, w_all, gb, bd, mod3, mod3, mod3, g)


def _ffn_up_kernel(h_ref, wa_ref, wb_ref, cwa_ref, cwb_ref, cba_ref, cbb_ref, o_ref):
    h = h_ref[...]
    seq = h.shape[0]
    tn = o_ref.shape[1]
    row = lax.broadcasted_iota(jnp.int32, (seq, 1), 0)
    subs = [(off, min(MXU_W, tn - off)) for off in range(0, tn, MXU_W)]

    def matmuls(sub):
        off, w = sub
        if w == MXU_W:
            return _dot(h, wa_ref[:, off:off + w]), _dot(h, wb_ref[:, off:off + w])
        u = _dot(h, jnp.concatenate([wa_ref[:, off:off + w], wb_ref[:, off:off + w]], axis=1))
        return u[:, :w], u[:, w:]

    def conv(u, cw, cb):
        prev = jnp.where(row == 0, 0.0, pltpu.roll(u, 1, 0))
        nxt = jnp.where(row == seq - 1, 0.0, pltpu.roll(u, seq - 1, 0))
        return cw[0:1, :] * prev + cw[1:2, :] * u + cw[2:3, :] * nxt + cb

    def gate(sub, us):
        off, w = sub
        cols = slice(off, off + w)
        a = conv(us[0], cwa_ref[:, cols], cba_ref[:, cols])
        b = conv(us[1], cwb_ref[:, cols], cbb_ref[:, cols])
        o_ref[:, cols] = (_silu(a) * b).astype(BF16)

    _run_ahead(subs, matmuls, gate, ahead=2)


def _ffn_up(h, w_up, conv_w, conv_b, seq):
    t, d = h.shape
    dff = w_up.shape[1] // 2
    tn = TN_FFN
    nt = dff // tn
    lo = lambda b, j: (0, j)
    hi = lambda b, j: (0, nt + j)
    return pl.pallas_call(
        _ffn_up_kernel,
        grid=(t // seq, nt),
        in_specs=[
            pl.BlockSpec((seq, d), lambda b, j: (b, 0)),
            pl.BlockSpec((d, tn), lo), pl.BlockSpec((d, tn), hi),
            pl.BlockSpec((CONV_W, tn), lo), pl.BlockSpec((CONV_W, tn), hi),
            pl.BlockSpec((1, tn), lo), pl.BlockSpec((1, tn), hi),
        ],
        out_specs=pl.BlockSpec((seq, tn), lambda b, j: (b, j)),
        out_shape=jax.ShapeDtypeStruct((t, dff), BF16),
        compiler_params=_cparams("arbitrary", "arbitrary"),
        name="ffn_up",
    )(h, w_up, w_up, conv_w, conv_w, conv_b, conv_b)


def _ffn_down_kernel(x_ref, a_ref, w_ref, g2_ref, o_ref, wb_ref):
    _cast_weights_once(w_ref, wb_ref)
    o_ref[...] = x_ref[...] + g2_ref[...] * _dot(a_ref[...], wb_ref[...])


def _ffn_down(x2, act, w_all, layer, mod3, seq):
    t, d = x2.shape
    tm = TM_PROJ
    per_seq = seq // tm
    row = lambda i: (i, 0)
    return pl.pallas_call(
        _ffn_down_kernel,
        grid=(t // tm,),
        in_specs=[
            pl.BlockSpec((tm, d), row),
            pl.BlockSpec((tm, act.shape[1]), row),
            _layer_weight_spec(w_all, layer),
            pl.BlockSpec((None, 1, d), lambda i: (i // per_seq, 0, 5)),
        ],
        out_specs=pl.BlockSpec((tm, d), row),
        out_shape=jax.ShapeDtypeStruct((t, d), F32),
        scratch_shapes=[pltpu.VMEM(w_all.shape[1:], BF16)],
        compiler_params=_cparams("arbitrary"),
        name="ffn_down",
    )(x2, act, w_all, mod3)


def _rope_tables(seq):
    n_rows = seq // GRID_W
    rowp = np.repeat(np.arange(n_rows), GRID_W).astype(np.float32)
    colp = np.tile(np.arange(GRID_W), n_rows).astype(np.float32)
    half = HEAD_DIM // 2
    inv = (np.float32(ROPE_THETA) ** (-np.arange(0, half, 2, dtype=np.float32) / half)).astype(np.float32)
    ang_r = rowp[:, None] * inv
    ang_c = colp[:, None] * inv
    cos = np.concatenate([np.cos(ang_r), np.cos(ang_r), np.cos(ang_c), np.cos(ang_c)], axis=1)
    sin = np.concatenate([-np.sin(ang_r), np.sin(ang_r), -np.sin(ang_c), np.sin(ang_c)], axis=1)
    tile = lambda a: jnp.asarray(np.tile(a.astype(np.float32), (1, A_HEADS)))
    return tile(cos), tile(sin)


def _head_block_ones(width):
    i = np.arange(width) // HEAD_DIM
    return jnp.asarray((i[:, None] == i[None, :]).astype(np.float32), dtype=BF16)


def kernel(x, c, w_ada, b_ada, norm_g, w_in, a_q_norm, a_k_norm, b_lb, b_out_norm, c_q_norm, c_k_norm,
           w_out, w_up, conv_w, conv_b, w_down):
    bsz, seq, d = x.shape
    depth = w_in.shape[0]
    t = bsz * seq
    scale = HEAD_DIM ** -0.5

    mod = _ada(c, w_ada, b_ada)
    lb_all = _hgrn_lb(b_lb.astype(F32))
    cos, sin = _rope_tables(seq)
    bd = _head_block_ones(A_W)
    tri = _hgrn_tri_constants()
    bias_strided, bias_local = _dilated_bias_tables(seq)
    tile = lambda gvec, heads, s=1.0: (jnp.tile(gvec.astype(F32), heads) * s).reshape(1, -1)

    x2 = x.reshape(t, d)
    for l in range(depth):
        mod3 = mod[l].reshape(bsz, 1, 6 * d)
        aqt, ak, avt, bmix, cq, ck, cv = _in_proj(
            x2, mod3, norm_g[l, 0].reshape(1, d), w_in, l, cos, sin,
            tile(a_q_norm[l], A_HEADS, scale), tile(a_k_norm[l], A_KV_HEADS),
            tile(c_q_norm[l], C_HEADS, scale), tile(c_k_norm[l], C_KV_HEADS), bd, seq)
        o_a = _attention(aqt, ak, avt, seq, n_heads=A_HEADS, n_kv=A_KV_HEADS)
        far = _dil_strided(cq, ck, cv, bias_strided, seq)
        o_c = _dil_local(cq, ck, cv, bias_local, far, seq)
        o_f, o_b = _hgrn(bmix, lb_all[0, l].reshape(1, B_W), lb_all[1, l].reshape(1, B_W), tri, seq)
        x2, h2 = _out_proj(x2, o_a, o_f, o_b, bmix, o_c, w_out, l,
                           tile(b_out_norm[l], B_HEADS), bd[:B_W, :B_W], mod3,
                           norm_g[l, 1].reshape(1, d), seq)
        act = _ffn_up(h2, w_up[l].astype(BF16), conv_w[l], conv_b[l].reshape(1, -1), seq)
        x2 = _ffn_down(x2, act, w_down, l, mod3, seq)
    return x2.reshape(bsz, seq, d)
```

```python
import functools

import numpy as np
import jax
import jax.numpy as jnp
from jax import lax
from jax.experimental import pallas as pl
from jax.experimental.pallas import tpu as pltpu

F32 = jnp.float32
BF16 = jnp.bfloat16

HEAD_DIM = 64
GRID_W = 64
EPS = 1e-6
NEG_BIG = -1e30
F_MIN = 1e-6
ROPE_THETA = 10000.0
A_HEADS, A_KV_HEADS = 6, 2
B_HEADS = 4
C_HEADS, C_KV_HEADS = 6, 2
C_BRANCHES = ((128, 1), (512, 4), (2048, 16))
CONV_W = 3

A_W = A_HEADS * HEAD_DIM
KV_W = A_KV_HEADS * HEAD_DIM
B_W = B_HEADS * HEAD_DIM
C_W = C_HEADS * HEAD_DIM
OFF_AQ, OFF_AK, OFF_AV = 0, A_W, A_W + KV_W
OFF_B = A_W + 2 * KV_W
OFF_CQ = OFF_B + 5 * B_W
OFF_CK, OFF_CV = OFF_CQ + C_W, OFF_CQ + C_W + KV_W
IN_CUT = OFF_CQ - KV_W
assert OFF_AV % 256 == 0 and IN_CUT % 256 == 0 and (OFF_CV + KV_W) % 256 == 0

TM_PROJ = 512
TQ_ATTN = 256
KEY_CHUNK = 256
SCORE_ROWS = 256
ATTN_AHEAD = 3
HG_CHUNK = 64
HG_GROUP = 4
HG_BLK = 8
HG_MID = 4
MXU_W = 256
TN_FFN = 1408
TN_ADA = 1536
VMEM_LIMIT = 56 * 1024 * 1024


def _cparams(*sem):
    return pltpu.CompilerParams(dimension_semantics=sem, vmem_limit_bytes=VMEM_LIMIT)


def _dot(a, b):
    return jnp.dot(a, b, preferred_element_type=F32)


def _dot_nt(a, b):
    return lax.dot_general(a, b, (((1,), (1,)), ((), ())), preferred_element_type=F32)


def _dot_tn(a, b):
    return lax.dot_general(a, b, (((0,), (0,)), ((), ())), preferred_element_type=F32)


def _split2(x):
    hi = x.astype(BF16)
    lo = (x - hi.astype(F32)).astype(BF16)
    return hi, lo


def _head_mean_sq(x, bd):
    hi, lo = _split2(x * x)
    return (_dot(hi, bd) + _dot(lo, bd)) * (1.0 / HEAD_DIM)


def _silu(x):
    return x * jax.nn.sigmoid(x)


def _cast_weights_once(w_ref, wb_ref):
    @pl.when(pl.program_id(0) == 0)
    def _():
        wb_ref[...] = w_ref[...].astype(BF16)


def _layer_weight_spec(w_all, layer):
    return pl.BlockSpec((None,) + w_all.shape[1:], lambda i: (layer, 0, 0))


def _ada_kernel(c_ref, w_ref, b_ref, o_ref):
    a = _silu(c_ref[...]).astype(BF16)
    o_ref[...] = _dot(a, w_ref[...].astype(BF16)) + b_ref[...]


def _ada(c, w_ada, b_ada):
    depth, d, n = w_ada.shape
    bsz = c.shape[0]
    return pl.pallas_call(
        _ada_kernel,
        grid=(depth, n // TN_ADA),
        in_specs=[
            pl.BlockSpec((bsz, d), lambda l, j: (0, 0)),
            pl.BlockSpec((None, d, TN_ADA), lambda l, j: (l, 0, j)),
            pl.BlockSpec((None, 1, TN_ADA), lambda l, j: (l, 0, j)),
        ],
        out_specs=pl.BlockSpec((None, bsz, TN_ADA), lambda l, j: (l, 0, j)),
        out_shape=jax.ShapeDtypeStruct((depth, bsz, n), F32),
        compiler_params=_cparams("arbitrary", "arbitrary"),
        name="ada",
    )(c, w_ada, b_ada.reshape(depth, 1, n))


def _lb_kernel(b_ref, o_ref):
    depth = b_ref.shape[1]
    for d in range(2):
        rows = [b_ref[d, l:l + 1, :] for l in range(depth)]
        m = functools.reduce(jnp.maximum, rows)
        e = [jnp.exp(r - m) for r in rows]
        tot = functools.reduce(lambda a, b: a + b, e)
        run = jnp.zeros_like(m)
        for l in range(depth):
            sm = e[l] / tot
            run = run + sm
            o_ref[d, l:l + 1, :] = run - e[0] / tot


def _hgrn_lb(b_lb):
    return pl.pallas_call(
        _lb_kernel,
        out_shape=jax.ShapeDtypeStruct(b_lb.shape, F32),
        name="hgrn_lb",
    )(b_lb)


def _rope(z, cos, sin):
    n = z.shape[-1]
    lane = lax.broadcasted_iota(jnp.int32, z.shape, 1)
    up = pltpu.roll(z, n - 16, 1)
    dn = pltpu.roll(z, 16, 1)
    return z * cos + jnp.where((lane % 32) < 16, up, dn) * sin


def _in_proj_kernel(x_ref, sh_ref, sc_ref, g_ref, w_ref, cos_ref, sin_ref,
                    gaq_ref, gak_ref, gcq_ref, gck_ref, bd_ref,
                    aqt_ref, ak_ref, avt_ref, bmix_ref, cq_ref, ck_ref, cv_ref, wb_ref):
    _cast_weights_once(w_ref, wb_ref)

    def headnorm(z, gain):
        n = z.shape[-1]
        return z * lax.rsqrt(_head_mean_sq(z, bd_ref[:n, :n]) + EPS) * gain

    x = x_ref[...]
    y = x * lax.rsqrt(jnp.mean(x * x, axis=-1, keepdims=True) + EPS) * g_ref[...]
    h = (y * (1.0 + sc_ref[...]) + sh_ref[...]).astype(BF16)

    cuts = (0, OFF_AV, IN_CUT, OFF_CV + KV_W)
    proj = lambda i: _dot(h, wb_ref[:, cuts[i]:cuts[i + 1]])
    cos, sin = cos_ref[...], sin_ref[...]
    d0 = proj(0)
    d1 = proj(1)
    aqt_ref[...] = _rope(headnorm(d0[:, :A_W], gaq_ref[...]), cos, sin).T.astype(BF16)
    ak_ref[...] = _rope(headnorm(d0[:, A_W:], gak_ref[...]), cos[:, :KV_W], sin[:, :KV_W]).astype(BF16)
    d2 = proj(2)
    avt_ref[...] = d1[:, :KV_W].T.astype(BF16)
    bmix_ref[:, :IN_CUT - OFF_B] = d1[:, KV_W:]
    bmix_ref[:, IN_CUT - OFF_B:] = d2[:, :OFF_CQ - IN_CUT]
    c0 = OFF_CQ - IN_CUT
    cq_ref[...] = headnorm(d2[:, c0:c0 + C_W], gcq_ref[...])
    ck_ref[...] = headnorm(d2[:, c0 + C_W:c0 + C_W + KV_W], gck_ref[...])
    cv_ref[...] = d2[:, c0 + C_W + KV_W:]


def _in_proj(x2, mod3, g, w_all, layer, cos, sin, gaq, gak, gcq, gck, bd, seq):
    t, d = x2.shape
    tm = TM_PROJ
    per_seq = seq // tm
    row = lambda i: (i, 0)
    const = lambda i: (0, 0)
    modspec = lambda k: pl.BlockSpec((None, 1, d), lambda i: (i // per_seq, 0, k))
    return pl.pallas_call(
        _in_proj_kernel,
        grid=(t // tm,),
        in_specs=[
            pl.BlockSpec((tm, d), row),
            modspec(0), modspec(1),
            pl.BlockSpec((1, d), const),
            _layer_weight_spec(w_all, layer),
            pl.BlockSpec((tm, A_W), lambda i: (i % per_seq, 0)),
            pl.BlockSpec((tm, A_W), lambda i: (i % per_seq, 0)),
            pl.BlockSpec((1, A_W), const), pl.BlockSpec((1, KV_W), const),
            pl.BlockSpec((1, C_W), const), pl.BlockSpec((1, KV_W), const),
            pl.BlockSpec(bd.shape, const),
        ],
        out_specs=[
            pl.BlockSpec((A_W, tm), lambda i: (0, i)),
            pl.BlockSpec((tm, KV_W), row),
            pl.BlockSpec((KV_W, tm), lambda i: (0, i)),
            pl.BlockSpec((tm, 5 * B_W), row),
            pl.BlockSpec((tm, C_W), row),
            pl.BlockSpec((tm, KV_W), row),
            pl.BlockSpec((tm, KV_W), row),
        ],
        out_shape=[
            jax.ShapeDtypeStruct((A_W, t), BF16),
            jax.ShapeDtypeStruct((t, KV_W), BF16),
            jax.ShapeDtypeStruct((KV_W, t), BF16),
            jax.ShapeDtypeStruct((t, 5 * B_W), F32),
            jax.ShapeDtypeStruct((t, C_W), F32),
            jax.ShapeDtypeStruct((t, KV_W), F32),
            jax.ShapeDtypeStruct((t, KV_W), F32),
        ],
        scratch_shapes=[pltpu.VMEM(w_all.shape[1:], BF16)],
        compiler_params=_cparams("arbitrary"),
        name="in_proj",
    )(x2, mod3, mod3, g, w_all, cos, sin, gaq, gak, gcq, gck, bd)


def _alibi_slopes():
    return [float(s) for s in (2.0 ** (-8.0 * np.arange(1, C_HEADS + 1) / C_HEADS)).astype(np.float32)]


def _run_ahead(items, start, finish, ahead=ATTN_AHEAD):
    pending = {}
    for k in range(len(items) + ahead):
        if k < len(items):
            pending[k] = start(items[k])
        if k >= ahead:
            finish(items[k - ahead], pending.pop(k - ahead))


def _attn_kernel(qt_ref, k_ref, vt_ref, o_ref, s_ref, *, n_heads, n_kv):
    group = n_heads // n_kv
    tq, kc = TQ_ATTN, KEY_CHUNK
    seq = k_ref.shape[0]
    zeros = jnp.zeros((HEAD_DIM, tq), BF16)
    fold = lambda x, op: op(x.reshape(kc // 8, 8, tq), axis=0)
    items = [(i, h) for i in range(qt_ref.shape[1] // tq) for h in range(n_heads)]
    outs = []

    def scores(item):
        i, h = item
        qt = qt_ref[h * HEAD_DIM:(h + 1) * HEAD_DIM, i * tq:(i + 1) * tq]
        qtz = jnp.concatenate([qt, zeros] if h // group == 0 else [zeros, qt], axis=0)
        slot = items.index(item) % s_ref.shape[0]
        for r in range(0, seq, SCORE_ROWS):
            s_ref[slot, r:r + SCORE_ROWS, :] = _dot(k_ref[r:r + SCORE_ROWS, :], qtz)
        return slot

    def finish(item, slot):
        i, h = item
        j = h // group
        m8 = fold(s_ref[slot, 0:kc, :], jnp.max)
        for r in range(kc, seq, kc):
            m8 = jnp.maximum(m8, fold(s_ref[slot, r:r + kc, :], jnp.max))
        m = jnp.max(m8, axis=0, keepdims=True)
        l8 = jnp.zeros((8, tq), F32)
        ot = jnp.zeros((HEAD_DIM, tq), F32)
        for r in range(0, seq, kc):
            p = jnp.exp(s_ref[slot, r:r + kc, :] - m)
            l8 = l8 + fold(p, jnp.sum)
            ot = ot + _dot(vt_ref[j * HEAD_DIM:(j + 1) * HEAD_DIM, r:r + kc], p.astype(BF16))
        outs.append(ot / jnp.sum(l8, axis=0, keepdims=True))
        if h == n_heads - 1:
            o_ref[i * tq:(i + 1) * tq, :] = jnp.concatenate(outs[-n_heads:], axis=0).T.astype(o_ref.dtype)

    _run_ahead(items, scores, finish)


def _attention(qt, k, vt, seq, *, n_heads, n_kv):
    qw, t = qt.shape
    assert n_kv == 2 and k.shape[1] == n_kv * HEAD_DIM
    return pl.pallas_call(
        functools.partial(_attn_kernel, n_heads=n_heads, n_kv=n_kv),
        grid=(t // seq,),
        in_specs=[
            pl.BlockSpec((qw, seq), lambda b: (0, b)),
            pl.BlockSpec((seq, k.shape[1]), lambda b: (b, 0)),
            pl.BlockSpec((vt.shape[0], seq), lambda b: (0, b)),
        ],
        out_specs=pl.BlockSpec((seq, qw), lambda b: (b, 0)),
        out_shape=jax.ShapeDtypeStruct((t, qw), BF16),
        scratch_shapes=[pltpu.VMEM((ATTN_AHEAD + 1, seq, TQ_ATTN), F32)],
        compiler_params=_cparams("arbitrary"),
        name="attn_rope",
    )(qt, k, vt)


DIL_RES = 4
DIL_TQ = 128
DIL_WIN = 3 * DIL_TQ
PAIR_W = 2 * HEAD_DIM
assert C_BRANCHES == ((128, 1), (512, 4), (2048, 16)) and C_BRANCHES[0][0] // 2 <= DIL_TQ


def _dilated_bias_tables(seq):
    slopes = jnp.asarray(_alibi_slopes(), F32).reshape(C_HEADS, 1, 1)
    n = seq // DIL_RES
    i = jnp.arange(n, dtype=jnp.int32)
    a = jnp.abs(i[None, :] - i[:, None])
    w2, w3 = C_BRANCHES[1][0] // (2 * DIL_RES), C_BRANCHES[2][0] // (2 * DIL_RES)
    step3 = C_BRANCHES[2][1] // DIL_RES
    count = (a <= w2).astype(jnp.int32) + ((a <= w3) & (a % step3 == 0)).astype(jnp.int32)
    log_mult = jnp.where(count == 2, float(np.log(2.0)), jnp.where(count == 1, 0.0, NEG_BIG))
    strided = log_mult[None] - slopes * (DIL_RES * a).astype(F32)[None]
    r = jnp.arange(DIL_TQ, dtype=jnp.int32)[:, None]
    c = jnp.arange(DIL_WIN, dtype=jnp.int32)[None, :]
    half = C_BRANCHES[0][0] // 2
    local = []
    for shift in (0, DIL_TQ, 2 * DIL_TQ):
        d = jnp.abs(c - shift - r)
        local.append(jnp.where((d <= half)[None], -slopes * d.astype(F32)[None], NEG_BIG))
    return strided, jnp.stack(local)


def _kv_variants(kt, v_f32):
    zeros = jnp.zeros((HEAD_DIM, kt.shape[1]), BF16)
    ktz = [[jnp.concatenate([kt[j * HEAD_DIM:(j + 1) * HEAD_DIM], zeros], axis=0) for j in range(2)],
           [jnp.concatenate([zeros, kt[j * HEAD_DIM:(j + 1) * HEAD_DIM]], axis=0) for j in range(2)]]
    v = v_f32.astype(BF16)
    vswap = pltpu.roll(v_f32, HEAD_DIM, 1).astype(BF16)
    vsel = [[v, vswap], [vswap, v]]
    return ktz, vsel


def _dil_strided_kernel(q0_ref, q1_ref, q2_ref, k_ref, v_ref, bias_ref,
                        o0_ref, o1_ref, o2_ref, l0_ref, l1_ref, l2_ref):
    n = k_ref.shape[0] // DIL_RES
    group = C_HEADS // C_KV_HEADS
    left = lax.broadcasted_iota(jnp.int32, (n, PAIR_W), 1) < HEAD_DIM
    slabs = ((q0_ref, o0_ref, l0_ref), (q1_ref, o1_ref, l1_ref), (q2_ref, o2_ref, l2_ref))
    rows = [pl.ds(r, n, stride=DIL_RES) for r in range(DIL_RES)]
    kv = [_kv_variants(k_ref[rows[r], :].T.astype(BF16), v_ref[rows[r], :]) for r in range(DIL_RES)]
    items = [(r, h) for r in range(DIL_RES) for h in range(C_HEADS)]
    res, lse = [], []

    def scores(item):
        r, h = item
        q = slabs[h // 2][0][rows[r], :].astype(BF16)
        return _dot(q, kv[r][0][h % 2][h // group]) + bias_ref[h]

    def finish(item, s):
        r, h = item
        m = jnp.max(s, axis=-1, keepdims=True)
        p = jnp.exp(s - m)
        l = jnp.sum(p, axis=-1, keepdims=True)
        res.append(_dot(p.astype(BF16), kv[r][1][h % 2][h // group]) / l)
        lse.append(m + jnp.log(l))
        if h % 2 == 1:
            _, o_ref, l_ref = slabs[h // 2]
            o_ref[rows[r], :] = jnp.where(left, res[-2], res[-1])
            l_ref[rows[r], :] = jnp.where(left, lse[-2], lse[-1])

    _run_ahead(items, scores, finish)


def _dil_strided(cq, ck, cv, bias, seq):
    t = cq.shape[0]
    blk = lambda col: pl.BlockSpec((seq, PAIR_W), lambda b: (b, col))
    slab = jax.ShapeDtypeStruct((t, PAIR_W), F32)
    return pl.pallas_call(
        _dil_strided_kernel,
        grid=(t // seq,),
        in_specs=[blk(0), blk(1), blk(2), blk(0), blk(0),
                  pl.BlockSpec(bias.shape, lambda b: (0, 0, 0))],
        out_specs=[blk(0)] * 6,
        out_shape=[slab] * 6,
        compiler_params=_cparams("arbitrary"),
        name="attn_dil_strided",
    )(cq, cq, cq, ck, cv, bias)


def _dil_local_kernel(q0_ref, q1_ref, q2_ref, k_ref, v_ref, bias_ref,
                      f0_ref, f1_ref, f2_ref, l0_ref, l1_ref, l2_ref, o_ref):
    seq = k_ref.shape[0]
    tq, win = DIL_TQ, DIL_WIN
    group = C_HEADS // C_KV_HEADS
    ktz, vsel = _kv_variants(k_ref[...].T.astype(BF16), v_ref[...])
    left = lax.broadcasted_iota(jnp.int32, (tq, PAIR_W), 1) < HEAD_DIM
    slabs = ((q0_ref, f0_ref, l0_ref), (q1_ref, f1_ref, l1_ref), (q2_ref, f2_ref, l2_ref))
    items = [(i, slab, half) for i in range(seq // tq) for slab in range(3) for half in range(2)]

    def window(i):
        t0 = i * tq
        w0 = min(max(t0 - tq, 0), seq - win)
        return t0, w0, (t0 - w0) // tq

    def scores(item):
        i, slab, half = item
        t0, w0, variant = window(i)
        h = 2 * slab + half
        q = slabs[slab][0][t0:t0 + tq, :].astype(BF16)
        return _dot(q, ktz[half][h // group][:, w0:w0 + win]) + bias_ref[variant, h]

    def finish(item, s, acc):
        i, slab, half = item
        t0, w0, _ = window(i)
        m = jnp.max(s, axis=-1, keepdims=True)
        p = jnp.exp(s - m)
        near = _dot(p.astype(BF16), vsel[half][(2 * slab + half) // group][w0:w0 + win, :])
        acc.append((near, m, jnp.sum(p, axis=-1, keepdims=True)))
        if half == 0:
            return
        (n0, m0, s0), (n1, m1, s1) = acc[-2], acc[-1]
        near, m_near, l_near = jnp.where(left, n0, n1), jnp.where(left, m0, m1), jnp.where(left, s0, s1)
        _, far_ref, lse_ref = slabs[slab]
        lse_far = lse_ref[t0:t0 + tq, :]
        top = jnp.maximum(m_near, lse_far)
        w_near = jnp.exp(m_near - top)
        w_far = jnp.exp(lse_far - top)
        merged = (near * w_near + far_ref[t0:t0 + tq, :] * w_far) / (l_near * w_near + w_far)
        o_ref[t0:t0 + tq, slab * PAIR_W:(slab + 1) * PAIR_W] = merged.astype(o_ref.dtype)

    acc = []
    _run_ahead(items, scores, lambda item, s: finish(item, s, acc))


def _dil_local(cq, ck, cv, bias, far, seq):
    t = cq.shape[0]
    blk = lambda col: pl.BlockSpec((seq, PAIR_W), lambda b: (b, col))
    return pl.pallas_call(
        _dil_local_kernel,
        grid=(t // seq,),
        in_specs=[blk(0), blk(1), blk(2), blk(0), blk(0),
                  pl.BlockSpec(bias.shape, lambda b: (0, 0, 0, 0))] + [blk(0)] * 6,
        out_specs=pl.BlockSpec((seq, C_W), lambda b: (b, 0)),
        out_shape=jax.ShapeDtypeStruct((t, C_W), BF16),
        compiler_params=_cparams("arbitrary"),
        name="attn_dil_local",
    )(cq, cq, cq, ck, cv, bias, *far)


def _hgrn_tri_constants():
    t = np.arange(HG_GROUP * HG_CHUNK)
    same_chunk = (t[None, :] // HG_CHUNK) == (t[:, None] // HG_CHUNK)
    lower = (t[None, :] <= t[:, None]) & same_chunk
    upper = (t[None, :] >= t[:, None]) & same_chunk
    return jnp.asarray(np.stack([lower, upper]).astype(np.float32), dtype=BF16)


def _split3_cols(x):
    hi = x.astype(BF16)
    r = x - hi.astype(F32)
    mid = r.astype(BF16)
    lo = (r - mid.astype(F32)).astype(BF16)
    return jnp.concatenate([hi, mid, lo], axis=1)


def _hgrn_prefix(q, fpre, v, lb, tri, reverse):
    rows, w = q.shape
    c = HG_CHUNK
    f = jnp.maximum(lb + (1.0 - lb) * jax.nn.sigmoid(fpre), F_MIN)
    lf = jnp.log(f)
    kk = 1.0 - f
    r = _dot(tri, _split3_cols(lf))
    b = r[:, 2 * w:] + r[:, w:2 * w] + r[:, :w]
    b3 = b.reshape(rows // HG_BLK, HG_BLK, w)
    bmid = jnp.broadcast_to(b3[:, HG_MID:HG_MID + 1, :], b3.shape).reshape(rows, w)
    blast = [b[g * c:g * c + 1] if reverse else b[(g + 1) * c - 1:(g + 1) * c] for g in range(rows // c)]
    blast_rows = jnp.concatenate([jnp.broadcast_to(x, (c, w)) for x in blast], axis=0)
    return dict(q=q, b=b, blast=blast, vb=v.astype(BF16), reverse=reverse,
                qe=(q * jnp.exp(b)).astype(BF16),
                kdec=(kk * jnp.exp(blast_rows - b)).astype(BF16),
                ktil=kk * jnp.exp(bmid - b))


def _hgrn_tilde(pre, rs, sl, blk_mask):
    qh, bh, reverse = pre["q"][rs, sl], pre["b"][rs, sl], pre["reverse"]
    c = qh.shape[0]
    nblk = c // HG_BLK
    pieces = []
    for j in range(nblk):
        ref_row = bh[j * HG_BLK + HG_MID:j * HG_BLK + HG_MID + 1, :]
        if reverse:
            hi_row = (j + 1) * HG_BLK
            qt = qh[:hi_row] * jnp.exp(bh[:hi_row] - ref_row)
            if hi_row < c:
                qt = jnp.concatenate([qt, jnp.zeros((c - hi_row, HEAD_DIM), F32)], axis=0)
        else:
            lo_row = j * HG_BLK
            qt = qh[lo_row:] * jnp.exp(bh[lo_row:] - ref_row)
            if lo_row > 0:
                qt = jnp.concatenate([jnp.zeros((lo_row, HEAD_DIM), F32), qt], axis=0)
        pieces.append(qt)
    qtil = jnp.concatenate(pieces, axis=1).astype(BF16)
    kt = jnp.concatenate([pre["ktil"][rs, sl]] * nblk, axis=1)
    return qtil, jnp.where(blk_mask, kt, 0.0).astype(BF16)


def _hgrn_kernel(qf_ref, ff_ref, vf_ref, qb_ref, fb_ref, vb_ref, lbf_ref, lbb_ref, tri_ref,
                 of_ref, ob_ref, sf_ref, sb_ref):
    @pl.when(pl.program_id(1) == 0)
    def _():
        sf_ref[...] = jnp.zeros_like(sf_ref)
        sb_ref[...] = jnp.zeros_like(sb_ref)

    c = HG_CHUNK
    ngrp = qf_ref.shape[0] // c
    nblk = c // HG_BLK
    row = lax.broadcasted_iota(jnp.int32, (c, c), 0)
    col = lax.broadcasted_iota(jnp.int32, (c, c), 1)
    brow = lax.broadcasted_iota(jnp.int32, (c, nblk * HEAD_DIM), 0) // HG_BLK
    bcol = lax.broadcasted_iota(jnp.int32, (c, nblk * HEAD_DIM), 1) // HEAD_DIM
    blk_mask = brow == bcol

    pres = [_hgrn_prefix(qf_ref[...], ff_ref[...], vf_ref[...], lbf_ref[...], tri_ref[0], False),
            _hgrn_prefix(qb_ref[...], fb_ref[...], vb_ref[...], lbb_ref[...], tri_ref[1], True)]
    order = [list(range(ngrp)), list(range(ngrp - 1, -1, -1))]
    units = [(d, g, h) for d in range(2) for g in order[d] for h in range(B_HEADS)]
    sl = lambda h: slice(h * HEAD_DIM, (h + 1) * HEAD_DIM)
    rs = lambda g: slice(g * c, (g + 1) * c)
    state_refs = (sf_ref, sb_ref)

    upd = {u: _dot_tn(pres[u[0]]["vb"][rs(u[1]), sl(u[2])], pres[u[0]]["kdec"][rs(u[1]), sl(u[2])])
           for u in units}
    attn = {}
    for d, g, h in units:
        qtil, kt = _hgrn_tilde(pres[d], rs(g), sl(h), blk_mask)
        attn[d, g, h] = _dot_nt(qtil, kt)
    inter = {}
    for d in range(2):
        for h in range(B_HEADS):
            st = state_refs[d][h]
            for g in order[d]:
                inter[d, g, h] = _dot_nt(pres[d]["qe"][rs(g), sl(h)], st.astype(BF16))
                st = st * jnp.exp(pres[d]["blast"][g][:, sl(h)]) + upd[d, g, h]
            state_refs[d][h] = st
    outs = [[[None] * B_HEADS for _ in range(ngrp)] for _ in range(2)]
    for d, g, h in units:
        keep = (col >= row) if pres[d]["reverse"] else (col <= row)
        a = jnp.where(keep, attn[d, g, h], 0.0).astype(BF16)
        outs[d][g][h] = _dot(a, pres[d]["vb"][rs(g), sl(h)]) + inter[d, g, h]
    for d, o_ref in enumerate((of_ref, ob_ref)):
        o_ref[...] = jnp.concatenate([jnp.concatenate(per_head, axis=1) for per_head in outs[d]], axis=0)


def _hgrn(bmix, lbf, lbb, tri, seq):
    t = bmix.shape[0]
    bsz = t // seq
    c = HG_GROUP * HG_CHUNK
    n = seq // c
    fwd = lambda k: pl.BlockSpec((c, B_W), lambda b, j: (b * n + j, k))
    bwd = lambda k: pl.BlockSpec((c, B_W), lambda b, j: (b * n + n - 1 - j, k))
    const2 = lambda b, j: (0, 0)
    return pl.pallas_call(
        _hgrn_kernel,
        grid=(bsz, n),
        in_specs=[fwd(0), fwd(1), fwd(3), bwd(0), bwd(2), bwd(3),
                  pl.BlockSpec((1, B_W), const2), pl.BlockSpec((1, B_W), const2),
                  pl.BlockSpec(tri.shape, lambda b, j: (0, 0, 0))],
        out_specs=[fwd(0), bwd(0)],
        out_shape=[jax.ShapeDtypeStruct((t, B_W), F32), jax.ShapeDtypeStruct((t, B_W), F32)],
        scratch_shapes=[pltpu.VMEM((B_HEADS, HEAD_DIM, HEAD_DIM), F32)] * 2,
        compiler_params=_cparams("arbitrary", "arbitrary"),
        name="hgrn",
    )(bmix, bmix, bmix, bmix, bmix, bmix, lbf, lbb, tri)


def _out_proj_kernel(x_ref, oa_ref, of_ref, ob_ref, bg_ref, oc_ref, w_ref, gb_ref, bd_ref,
                     g1_ref, sh_ref, sc_ref, g_ref, xo_ref, h_ref, wb_ref):
    _cast_weights_once(w_ref, wb_ref)
    ob = of_ref[...] + ob_ref[...]
    ob = ob * lax.rsqrt(_head_mean_sq(ob, bd_ref[...]) + EPS) * gb_ref[...]
    ob = (ob * _silu(bg_ref[...])).astype(BF16)
    mix = (_dot(oa_ref[...], wb_ref[:A_W, :]) + _dot(ob, wb_ref[A_W:A_W + B_W, :])
           + _dot(oc_ref[...], wb_ref[A_W + B_W:, :]))
    x = x_ref[...] + g1_ref[...] * mix
    xo_ref[...] = x
    y = x * lax.rsqrt(jnp.mean(x * x, axis=-1, keepdims=True) + EPS) * g_ref[...]
    h_ref[...] = (y * (1.0 + sc_ref[...]) + sh_ref[...]).astype(BF16)


def _out_proj(x2, oa, of, ob, bmix, oc, w_all, layer, gb, bd, mod3, g, seq):
    t, d = x2.shape
    tm = TM_PROJ
    per_seq = seq // tm
    row = lambda i: (i, 0)
    const = lambda i: (0, 0)
    modspec = lambda k: pl.BlockSpec((None, 1, d), lambda i: (i // per_seq, 0, k))
    return pl.pallas_call(
        _out_proj_kernel,
        grid=(t // tm,),
        in_specs=[
            pl.BlockSpec((tm, d), row),
            pl.BlockSpec((tm, A_W), row),
            pl.BlockSpec((tm, B_W), row), pl.BlockSpec((tm, B_W), row),
            pl.BlockSpec((tm, B_W), lambda i: (i, 4)),
            pl.BlockSpec((tm, C_W), row),
            _layer_weight_spec(w_all, layer),
            pl.BlockSpec((1, B_W), const),
            pl.BlockSpec(bd.shape, const),
            modspec(2), modspec(3), modspec(4),
            pl.BlockSpec((1, d), const),
        ],
        out_specs=[pl.BlockSpec((tm, d), row), pl.BlockSpec((tm, d), row)],
        out_shape=[jax.ShapeDtypeStruct((t, d), F32), jax.ShapeDtypeStruct((t, d), BF16)],
        scratch_shapes=[pltpu.VMEM(w_all.shape[1:], BF16)],
        compiler_params=_cparams("arbitrary"),
        name="out_proj",
    )(x2, oa, of, ob, bmix, oc, w_all, gb, bd, mod3, mod3, mod3, g)


def _ffn_up_kernel(h_ref, wa_ref, wb_ref, cwa_ref, cwb_ref, cba_ref, cbb_ref, o_ref):
    h = h_ref[...]
    seq = h.shape[0]
    tn = o_ref.shape[1]
    row = lax.broadcasted_iota(jnp.int32, (seq, 1), 0)
    subs = [(off, min(MXU_W, tn - off)) for off in range(0, tn, MXU_W)]

    def matmuls(sub):
        off, w = sub
        if w == MXU_W:
            return _dot(h, wa_ref[:, off:off + w]), _dot(h, wb_ref[:, off:off + w])
        u = _dot(h, jnp.concatenate([wa_ref[:, off:off + w], wb_ref[:, off:off + w]], axis=1))
        return u[:, :w], u[:, w:]

    def conv(u, cw, cb):
        prev = jnp.where(row == 0, 0.0, pltpu.roll(u, 1, 0))
        nxt = jnp.where(row == seq - 1, 0.0, pltpu.roll(u, seq - 1, 0))
        return cw[0:1, :] * prev + cw[1:2, :] * u + cw[2:3, :] * nxt + cb

    def gate(sub, us):
        off, w = sub
        cols = slice(off, off + w)
        a = conv(us[0], cwa_ref[:, cols], cba_ref[:, cols])
        b = conv(us[1], cwb_ref[:, cols], cbb_ref[:, cols])
        o_ref[:, cols] = (_silu(a) * b).astype(BF16)

    _run_ahead(subs, matmuls, gate, ahead=2)


def _ffn_up(h, w_up, conv_w, conv_b, layer, seq):
    t, d = h.shape
    dff = w_up.shape[2] // 2
    tn = TN_FFN
    nt = dff // tn
    lo = lambda b, j: (layer, 0, j)
    hi = lambda b, j: (layer, 0, nt + j)
    return pl.pallas_call(
        _ffn_up_kernel,
        grid=(t // seq, nt),
        in_specs=[
            pl.BlockSpec((seq, d), lambda b, j: (b, 0)),
            pl.BlockSpec((None, d, tn), lo), pl.BlockSpec((None, d, tn), hi),
            pl.BlockSpec((None, CONV_W, tn), lo), pl.BlockSpec((None, CONV_W, tn), hi),
            pl.BlockSpec((None, 1, tn), lo), pl.BlockSpec((None, 1, tn), hi),
        ],
        out_specs=pl.BlockSpec((seq, tn), lambda b, j: (b, j)),
        out_shape=jax.ShapeDtypeStruct((t, dff), BF16),
        compiler_params=_cparams("arbitrary", "arbitrary"),
        name="ffn_up",
    )(h, w_up, w_up, conv_w, conv_w, conv_b, conv_b)


def _ffn_down_kernel(x_ref, a_ref, w_ref, g2_ref, o_ref, wb_ref):
    _cast_weights_once(w_ref, wb_ref)
    o_ref[...] = x_ref[...] + g2_ref[...] * _dot(a_ref[...], wb_ref[...])


def _ffn_down(x2, act, w_all, layer, mod3, seq):
    t, d = x2.shape
    tm = TM_PROJ
    per_seq = seq // tm
    row = lambda i: (i, 0)
    return pl.pallas_call(
        _ffn_down_kernel,
        grid=(t // tm,),
        in_specs=[
            pl.BlockSpec((tm, d), row),
            pl.BlockSpec((tm, act.shape[1]), row),
            _layer_weight_spec(w_all, layer),
            pl.BlockSpec((None, 1, d), lambda i: (i // per_seq, 0, 5)),
        ],
        out_specs=pl.BlockSpec((tm, d), row),
        out_shape=jax.ShapeDtypeStruct((t, d), F32),
        scratch_shapes=[pltpu.VMEM(w_all.shape[1:], BF16)],
        compiler_params=_cparams("arbitrary"),
        name="ffn_down",
    )(x2, act, w_all, mod3)


def _rope_tables(seq):
    n_rows = seq // GRID_W
    rowp = np.repeat(np.arange(n_rows), GRID_W).astype(np.float32)
    colp = np.tile(np.arange(GRID_W), n_rows).astype(np.float32)
    half = HEAD_DIM // 2
    inv = (np.float32(ROPE_THETA) ** (-np.arange(0, half, 2, dtype=np.float32) / half)).astype(np.float32)
    ang_r = rowp[:, None] * inv
    ang_c = colp[:, None] * inv
    cos = np.concatenate([np.cos(ang_r), np.cos(ang_r), np.cos(ang_c), np.cos(ang_c)], axis=1)
    sin = np.concatenate([-np.sin(ang_r), np.sin(ang_r), -np.sin(ang_c), np.sin(ang_c)], axis=1)
    tile = lambda a: jnp.asarray(np.tile(a.astype(np.float32), (1, A_HEADS)))
    return tile(cos), tile(sin)


def _head_block_ones(width):
    i = np.arange(width) // HEAD_DIM
    return jnp.asarray((i[:, None] == i[None, :]).astype(np.float32), dtype=BF16)


def kernel(x, c, w_ada, b_ada, norm_g, w_in, a_q_norm, a_k_norm, b_lb, b_out_norm, c_q_norm, c_k_norm,
           w_out, w_up, conv_w, conv_b, w_down):
    bsz, seq, d = x.shape
    depth = w_in.shape[0]
    t = bsz * seq
    scale = HEAD_DIM ** -0.5

    mod = _ada(c, w_ada, b_ada)
    lb_all = _hgrn_lb(b_lb.astype(F32))
    cos, sin = _rope_tables(seq)
    bd = _head_block_ones(A_W)
    tri = _hgrn_tri_constants()
    bias_strided, bias_local = _dilated_bias_tables(seq)
    tile = lambda gvec, heads, s=1.0: (jnp.tile(gvec.astype(F32), heads) * s).reshape(1, -1)
    w_up_bf16 = w_up.astype(BF16)
    conv_b3 = conv_b.reshape(depth, 1, -1)

    x2 = x.reshape(t, d)
    for l in range(depth):
        mod3 = mod[l].reshape(bsz, 1, 6 * d)
        aqt, ak, avt, bmix, cq, ck, cv = _in_proj(
            x2, mod3, norm_g[l, 0].reshape(1, d), w_in, l, cos, sin,
            tile(a_q_norm[l], A_HEADS, scale), tile(a_k_norm[l], A_KV_HEADS),
            tile(c_q_norm[l], C_HEADS, scale), tile(c_k_norm[l], C_KV_HEADS), bd, seq)
        o_a = _attention(aqt, ak, avt, seq, n_heads=A_HEADS, n_kv=A_KV_HEADS)
        far = _dil_strided(cq, ck, cv, bias_strided, seq)
        o_c = _dil_local(cq, ck, cv, bias_local, far, seq)
        o_f, o_b = _hgrn(bmix, lb_all[0, l].reshape(1, B_W), lb_all[1, l].reshape(1, B_W), tri, seq)
        x2, h2 = _out_proj(x2, o_a, o_f, o_b, bmix, o_c, w_out, l,
                           tile(b_out_norm[l], B_HEADS), bd[:B_W, :B_W], mod3,
                           norm_g[l, 1].reshape(1, d), seq)
        act = _ffn_up(h2, w_up_bf16, conv_w, conv_b3, l, seq)
        x2 = _ffn_down(x2, act, w_down, l, mod3, seq)
    return x2.reshape(bsz, seq, d)
```

```python
import functools

import numpy as np
import jax
import jax.numpy as jnp
from jax import lax
from jax.experimental import pallas as pl
from jax.experimental.pallas import tpu as pltpu

F32 = jnp.float32
BF16 = jnp.bfloat16

HEAD_DIM = 64
GRID_W = 64
EPS = 1e-6
NEG_BIG = -1e30
F_MIN = 1e-6
ROPE_THETA = 10000.0
A_HEADS, A_KV_HEADS = 6, 2
B_HEADS = 4
C_HEADS, C_KV_HEADS = 6, 2
C_BRANCHES = ((128, 1), (512, 4), (2048, 16))
CONV_W = 3

A_W = A_HEADS * HEAD_DIM
KV_W = A_KV_HEADS * HEAD_DIM
B_W = B_HEADS * HEAD_DIM
C_W = C_HEADS * HEAD_DIM
OFF_AQ, OFF_AK, OFF_AV = 0, A_W, A_W + KV_W
OFF_B = A_W + 2 * KV_W
OFF_CQ = OFF_B + 5 * B_W
OFF_CK, OFF_CV = OFF_CQ + C_W, OFF_CQ + C_W + KV_W
IN_CUT = OFF_CQ - KV_W
assert OFF_AV % 256 == 0 and IN_CUT % 256 == 0 and (OFF_CV + KV_W) % 256 == 0

TM_PROJ = 1024
TQ_ATTN = 256
KEY_CHUNK = 256
SCORE_ROWS = 256
ATTN_AHEAD = 3
HG_CHUNK = 64
HG_GROUP = 4
HG_BLK = 8
HG_MID = 4
MXU_W = 256
TN_FFN = 1408
TN_ADA = 1536
VMEM_LIMIT = 56 * 1024 * 1024


def _cparams(*sem):
    return pltpu.CompilerParams(dimension_semantics=sem, vmem_limit_bytes=VMEM_LIMIT)


def _dot(a, b):
    return jnp.dot(a, b, preferred_element_type=F32)


def _dot_nt(a, b):
    return lax.dot_general(a, b, (((1,), (1,)), ((), ())), preferred_element_type=F32)


def _dot_tn(a, b):
    return lax.dot_general(a, b, (((0,), (0,)), ((), ())), preferred_element_type=F32)


def _split2(x):
    hi = x.astype(BF16)
    lo = (x - hi.astype(F32)).astype(BF16)
    return hi, lo


def _head_mean_sq(x, bd):
    hi, lo = _split2(x * x)
    return (_dot(hi, bd) + _dot(lo, bd)) * (1.0 / HEAD_DIM)


def _silu(x):
    return x * jax.nn.sigmoid(x)


def _cast_weights_once(w_ref, wb_ref):
    @pl.when(pl.program_id(0) == 0)
    def _():
        wb_ref[...] = w_ref[...].astype(BF16)


def _layer_weight_spec(w_all, layer):
    return pl.BlockSpec((None,) + w_all.shape[1:], lambda i: (layer, 0, 0))


def _ada_kernel(c_ref, w_ref, b_ref, o_ref):
    a = _silu(c_ref[...]).astype(BF16)
    o_ref[...] = _dot(a, w_ref[...].astype(BF16)) + b_ref[...]


def _ada(c, w_ada, b_ada):
    depth, d, n = w_ada.shape
    bsz = c.shape[0]
    return pl.pallas_call(
        _ada_kernel,
        grid=(depth, n // TN_ADA),
        in_specs=[
            pl.BlockSpec((bsz, d), lambda l, j: (0, 0)),
            pl.BlockSpec((None, d, TN_ADA), lambda l, j: (l, 0, j)),
            pl.BlockSpec((None, 1, TN_ADA), lambda l, j: (l, 0, j)),
        ],
        out_specs=pl.BlockSpec((None, bsz, TN_ADA), lambda l, j: (l, 0, j)),
        out_shape=jax.ShapeDtypeStruct((depth, bsz, n), F32),
        compiler_params=_cparams("arbitrary", "arbitrary"),
        name="ada",
    )(c, w_ada, b_ada.reshape(depth, 1, n))


def _lb_kernel(b_ref, o_ref):
    depth = b_ref.shape[1]
    for d in range(2):
        rows = [b_ref[d, l:l + 1, :] for l in range(depth)]
        m = functools.reduce(jnp.maximum, rows)
        e = [jnp.exp(r - m) for r in rows]
        tot = functools.reduce(lambda a, b: a + b, e)
        run = jnp.zeros_like(m)
        for l in range(depth):
            sm = e[l] / tot
            run = run + sm
            o_ref[d, l:l + 1, :] = run - e[0] / tot


def _hgrn_lb(b_lb):
    return pl.pallas_call(
        _lb_kernel,
        out_shape=jax.ShapeDtypeStruct(b_lb.shape, F32),
        name="hgrn_lb",
    )(b_lb)


def _rope(z, cos, sin):
    n = z.shape[-1]
    lane = lax.broadcasted_iota(jnp.int32, z.shape, 1)
    up = pltpu.roll(z, n - 16, 1)
    dn = pltpu.roll(z, 16, 1)
    return z * cos + jnp.where((lane % 32) < 16, up, dn) * sin


def _in_proj_kernel(x_ref, sh_ref, sc_ref, g_ref, w_ref, cos_ref, sin_ref,
                    gaq_ref, gak_ref, gcq_ref, gck_ref, bd_ref,
                    aqt_ref, ak_ref, avt_ref, bmix_ref, cq_ref, ck_ref, cv_ref, wb_ref):
    _cast_weights_once(w_ref, wb_ref)

    def headnorm(z, gain):
        n = z.shape[-1]
        return z * lax.rsqrt(_head_mean_sq(z, bd_ref[:n, :n]) + EPS) * gain

    x = x_ref[...]
    y = x * lax.rsqrt(jnp.mean(x * x, axis=-1, keepdims=True) + EPS) * g_ref[...]
    h = (y * (1.0 + sc_ref[...]) + sh_ref[...]).astype(BF16)

    cuts = (0, OFF_AV, IN_CUT, OFF_CV + KV_W)
    proj = lambda i: _dot(h, wb_ref[:, cuts[i]:cuts[i + 1]])
    cos, sin = cos_ref[...], sin_ref[...]
    d0 = proj(0)
    d1 = proj(1)
    aqt_ref[...] = _rope(headnorm(d0[:, :A_W], gaq_ref[...]), cos, sin).T.astype(BF16)
    ak_ref[...] = _rope(headnorm(d0[:, A_W:], gak_ref[...]), cos[:, :KV_W], sin[:, :KV_W]).astype(BF16)
    d2 = proj(2)
    avt_ref[...] = d1[:, :KV_W].T.astype(BF16)
    bmix_ref[:, :IN_CUT - OFF_B] = d1[:, KV_W:]
    bmix_ref[:, IN_CUT - OFF_B:] = d2[:, :OFF_CQ - IN_CUT]
    c0 = OFF_CQ - IN_CUT
    cq_ref[...] = headnorm(d2[:, c0:c0 + C_W], gcq_ref[...])
    ck_ref[...] = headnorm(d2[:, c0 + C_W:c0 + C_W + KV_W], gck_ref[...])
    cv_ref[...] = d2[:, c0 + C_W + KV_W:]


def _in_proj(x2, mod3, g, w_all, layer, cos, sin, gaq, gak, gcq, gck, bd, seq):
    t, d = x2.shape
    tm = TM_PROJ
    per_seq = seq // tm
    row = lambda i: (i, 0)
    const = lambda i: (0, 0)
    modspec = lambda k: pl.BlockSpec((None, 1, d), lambda i: (i // per_seq, 0, k))
    return pl.pallas_call(
        _in_proj_kernel,
        grid=(t // tm,),
        in_specs=[
            pl.BlockSpec((tm, d), row),
            modspec(0), modspec(1),
            pl.BlockSpec((1, d), const),
            _layer_weight_spec(w_all, layer),
            pl.BlockSpec((tm, A_W), lambda i: (i % per_seq, 0)),
            pl.BlockSpec((tm, A_W), lambda i: (i % per_seq, 0)),
            pl.BlockSpec((1, A_W), const), pl.BlockSpec((1, KV_W), const),
            pl.BlockSpec((1, C_W), const), pl.BlockSpec((1, KV_W), const),
            pl.BlockSpec(bd.shape, const),
        ],
        out_specs=[
            pl.BlockSpec((A_W, tm), lambda i: (0, i)),
            pl.BlockSpec((tm, KV_W), row),
            pl.BlockSpec((KV_W, tm), lambda i: (0, i)),
            pl.BlockSpec((tm, 5 * B_W), row),
            pl.BlockSpec((tm, C_W), row),
            pl.BlockSpec((tm, KV_W), row),
            pl.BlockSpec((tm, KV_W), row),
        ],
        out_shape=[
            jax.ShapeDtypeStruct((A_W, t), BF16),
            jax.ShapeDtypeStruct((t, KV_W), BF16),
            jax.ShapeDtypeStruct((KV_W, t), BF16),
            jax.ShapeDtypeStruct((t, 5 * B_W), F32),
            jax.ShapeDtypeStruct((t, C_W), F32),
            jax.ShapeDtypeStruct((t, KV_W), F32),
            jax.ShapeDtypeStruct((t, KV_W), F32),
        ],
        scratch_shapes=[pltpu.VMEM(w_all.shape[1:], BF16)],
        compiler_params=_cparams("arbitrary"),
        name="in_proj",
    )(x2, mod3, mod3, g, w_all, cos, sin, gaq, gak, gcq, gck, bd)


def _alibi_slopes():
    return [float(s) for s in (2.0 ** (-8.0 * np.arange(1, C_HEADS + 1) / C_HEADS)).astype(np.float32)]


def _run_ahead(items, start, finish, ahead=ATTN_AHEAD):
    pending = {}
    for k in range(len(items) + ahead):
        if k < len(items):
            pending[k] = start(items[k])
        if k >= ahead:
            finish(items[k - ahead], pending.pop(k - ahead))


def _attn_kernel(qt_ref, k_ref, vt_ref, o_ref, s_ref, *, n_heads, n_kv):
    group = n_heads // n_kv
    tq, kc = TQ_ATTN, KEY_CHUNK
    seq = k_ref.shape[0]
    zeros = jnp.zeros((HEAD_DIM, tq), BF16)
    fold = lambda x, op: op(x.reshape(kc // 8, 8, tq), axis=0)
    items = [(i, h) for i in range(qt_ref.shape[1] // tq) for h in range(n_heads)]
    outs = []

    def scores(item):
        i, h = item
        qt = qt_ref[h * HEAD_DIM:(h + 1) * HEAD_DIM, i * tq:(i + 1) * tq]
        qtz = jnp.concatenate([qt, zeros] if h // group == 0 else [zeros, qt], axis=0)
        slot = items.index(item) % s_ref.shape[0]
        for r in range(0, seq, SCORE_ROWS):
            s_ref[slot, r:r + SCORE_ROWS, :] = _dot(k_ref[r:r + SCORE_ROWS, :], qtz)
        return slot

    def finish(item, slot):
        i, h = item
        j = h // group
        m8 = fold(s_ref[slot, 0:kc, :], jnp.max)
        for r in range(kc, seq, kc):
            m8 = jnp.maximum(m8, fold(s_ref[slot, r:r + kc, :], jnp.max))
        m = jnp.max(m8, axis=0, keepdims=True)
        l8 = jnp.zeros((8, tq), F32)
        ot = jnp.zeros((HEAD_DIM, tq), F32)
        for r in range(0, seq, kc):
            p = jnp.exp(s_ref[slot, r:r + kc, :] - m)
            l8 = l8 + fold(p, jnp.sum)
            ot = ot + _dot(vt_ref[j * HEAD_DIM:(j + 1) * HEAD_DIM, r:r + kc], p.astype(BF16))
        outs.append(ot / jnp.sum(l8, axis=0, keepdims=True))
        if h == n_heads - 1:
            o_ref[i * tq:(i + 1) * tq, :] = jnp.concatenate(outs[-n_heads:], axis=0).T.astype(o_ref.dtype)

    _run_ahead(items, scores, finish)


def _attention(qt, k, vt, seq, *, n_heads, n_kv):
    qw, t = qt.shape
    assert n_kv == 2 and k.shape[1] == n_kv * HEAD_DIM
    return pl.pallas_call(
        functools.partial(_attn_kernel, n_heads=n_heads, n_kv=n_kv),
        grid=(t // seq,),
        in_specs=[
            pl.BlockSpec((qw, seq), lambda b: (0, b)),
            pl.BlockSpec((seq, k.shape[1]), lambda b: (b, 0)),
            pl.BlockSpec((vt.shape[0], seq), lambda b: (0, b)),
        ],
        out_specs=pl.BlockSpec((seq, qw), lambda b: (b, 0)),
        out_shape=jax.ShapeDtypeStruct((t, qw), BF16),
        scratch_shapes=[pltpu.VMEM((ATTN_AHEAD + 1, seq, TQ_ATTN), F32)],
        compiler_params=_cparams("arbitrary"),
        name="attn_rope",
    )(qt, k, vt)


DIL_RES = 4
DIL_TQ = 128
DIL_WIN = 3 * DIL_TQ
PAIR_W = 2 * HEAD_DIM
assert C_BRANCHES == ((128, 1), (512, 4), (2048, 16)) and C_BRANCHES[0][0] // 2 <= DIL_TQ


def _dilated_bias_tables(seq):
    slopes = jnp.asarray(_alibi_slopes(), F32).reshape(C_HEADS, 1, 1)
    n = seq // DIL_RES
    i = jnp.arange(n, dtype=jnp.int32)
    a = jnp.abs(i[None, :] - i[:, None])
    w2, w3 = C_BRANCHES[1][0] // (2 * DIL_RES), C_BRANCHES[2][0] // (2 * DIL_RES)
    step3 = C_BRANCHES[2][1] // DIL_RES
    count = (a <= w2).astype(jnp.int32) + ((a <= w3) & (a % step3 == 0)).astype(jnp.int32)
    log_mult = jnp.where(count == 2, float(np.log(2.0)), jnp.where(count == 1, 0.0, NEG_BIG))
    strided = log_mult[None] - slopes * (DIL_RES * a).astype(F32)[None]
    r = jnp.arange(DIL_TQ, dtype=jnp.int32)[:, None]
    c = jnp.arange(DIL_WIN, dtype=jnp.int32)[None, :]
    half = C_BRANCHES[0][0] // 2
    local = []
    for shift in (0, DIL_TQ, 2 * DIL_TQ):
        d = jnp.abs(c - shift - r)
        local.append(jnp.where((d <= half)[None], -slopes * d.astype(F32)[None], NEG_BIG))
    return strided, jnp.stack(local)


def _kv_variants(kt, v_f32):
    zeros = jnp.zeros((HEAD_DIM, kt.shape[1]), BF16)
    ktz = [[jnp.concatenate([kt[j * HEAD_DIM:(j + 1) * HEAD_DIM], zeros], axis=0) for j in range(2)],
           [jnp.concatenate([zeros, kt[j * HEAD_DIM:(j + 1) * HEAD_DIM]], axis=0) for j in range(2)]]
    v = v_f32.astype(BF16)
    vswap = pltpu.roll(v_f32, HEAD_DIM, 1).astype(BF16)
    vsel = [[v, vswap], [vswap, v]]
    return ktz, vsel


def _dil_strided_kernel(q0_ref, q1_ref, q2_ref, k_ref, v_ref, bias_ref,
                        o0_ref, o1_ref, o2_ref, l0_ref, l1_ref, l2_ref):
    n = k_ref.shape[0] // DIL_RES
    group = C_HEADS // C_KV_HEADS
    left = lax.broadcasted_iota(jnp.int32, (n, PAIR_W), 1) < HEAD_DIM
    slabs = ((q0_ref, o0_ref, l0_ref), (q1_ref, o1_ref, l1_ref), (q2_ref, o2_ref, l2_ref))
    rows = [pl.ds(r, n, stride=DIL_RES) for r in range(DIL_RES)]
    kv = [_kv_variants(k_ref[rows[r], :].T.astype(BF16), v_ref[rows[r], :]) for r in range(DIL_RES)]
    items = [(r, h) for r in range(DIL_RES) for h in range(C_HEADS)]
    res, lse = [], []

    def scores(item):
        r, h = item
        q = slabs[h // 2][0][rows[r], :].astype(BF16)
        return _dot(q, kv[r][0][h % 2][h // group]) + bias_ref[h]

    def finish(item, s):
        r, h = item
        m = jnp.max(s, axis=-1, keepdims=True)
        p = jnp.exp(s - m)
        l = jnp.sum(p, axis=-1, keepdims=True)
        res.append(_dot(p.astype(BF16), kv[r][1][h % 2][h // group]) / l)
        lse.append(m + jnp.log(l))
        if h % 2 == 1:
            _, o_ref, l_ref = slabs[h // 2]
            o_ref[rows[r], :] = jnp.where(left, res[-2], res[-1])
            l_ref[rows[r], :] = jnp.where(left, lse[-2], lse[-1])

    _run_ahead(items, scores, finish)


def _dil_strided(cq, ck, cv, bias, seq):
    t = cq.shape[0]
    blk = lambda col: pl.BlockSpec((seq, PAIR_W), lambda b: (b, col))
    slab = jax.ShapeDtypeStruct((t, PAIR_W), F32)
    return pl.pallas_call(
        _dil_strided_kernel,
        grid=(t // seq,),
        in_specs=[blk(0), blk(1), blk(2), blk(0), blk(0),
                  pl.BlockSpec(bias.shape, lambda b: (0, 0, 0))],
        out_specs=[blk(0)] * 6,
        out_shape=[slab] * 6,
        compiler_params=_cparams("arbitrary"),
        name="attn_dil_strided",
    )(cq, cq, cq, ck, cv, bias)


def _dil_local_kernel(q0_ref, q1_ref, q2_ref, k_ref, v_ref, bias_ref,
                      f0_ref, f1_ref, f2_ref, l0_ref, l1_ref, l2_ref, o_ref):
    seq = k_ref.shape[0]
    tq, win = DIL_TQ, DIL_WIN
    group = C_HEADS // C_KV_HEADS
    ktz, vsel = _kv_variants(k_ref[...].T.astype(BF16), v_ref[...])
    left = lax.broadcasted_iota(jnp.int32, (tq, PAIR_W), 1) < HEAD_DIM
    slabs = ((q0_ref, f0_ref, l0_ref), (q1_ref, f1_ref, l1_ref), (q2_ref, f2_ref, l2_ref))
    items = [(i, slab, half) for i in range(seq // tq) for slab in range(3) for half in range(2)]

    def window(i):
        t0 = i * tq
        w0 = min(max(t0 - tq, 0), seq - win)
        return t0, w0, (t0 - w0) // tq

    def scores(item):
        i, slab, half = item
        t0, w0, variant = window(i)
        h = 2 * slab + half
        q = slabs[slab][0][t0:t0 + tq, :].astype(BF16)
        return _dot(q, ktz[half][h // group][:, w0:w0 + win]) + bias_ref[variant, h]

    def finish(item, s, acc):
        i, slab, half = item
        t0, w0, _ = window(i)
        m = jnp.max(s, axis=-1, keepdims=True)
        p = jnp.exp(s - m)
        near = _dot(p.astype(BF16), vsel[half][(2 * slab + half) // group][w0:w0 + win, :])
        acc.append((near, m, jnp.sum(p, axis=-1, keepdims=True)))
        if half == 0:
            return
        (n0, m0, s0), (n1, m1, s1) = acc[-2], acc[-1]
        near, m_near, l_near = jnp.where(left, n0, n1), jnp.where(left, m0, m1), jnp.where(left, s0, s1)
        _, far_ref, lse_ref = slabs[slab]
        lse_far = lse_ref[t0:t0 + tq, :]
        top = jnp.maximum(m_near, lse_far)
        w_near = jnp.exp(m_near - top)
        w_far = jnp.exp(lse_far - top)
        merged = (near * w_near + far_ref[t0:t0 + tq, :] * w_far) / (l_near * w_near + w_far)
        o_ref[t0:t0 + tq, slab * PAIR_W:(slab + 1) * PAIR_W] = merged.astype(o_ref.dtype)

    acc = []
    _run_ahead(items, scores, lambda item, s: finish(item, s, acc))


def _dil_local(cq, ck, cv, bias, far, seq):
    t = cq.shape[0]
    blk = lambda col: pl.BlockSpec((seq, PAIR_W), lambda b: (b, col))
    return pl.pallas_call(
        _dil_local_kernel,
        grid=(t // seq,),
        in_specs=[blk(0), blk(1), blk(2), blk(0), blk(0),
                  pl.BlockSpec(bias.shape, lambda b: (0, 0, 0, 0))] + [blk(0)] * 6,
        out_specs=pl.BlockSpec((seq, C_W), lambda b: (b, 0)),
        out_shape=jax.ShapeDtypeStruct((t, C_W), BF16),
        compiler_params=_cparams("arbitrary"),
        name="attn_dil_local",
    )(cq, cq, cq, ck, cv, bias, *far)


def _hgrn_tri_constants():
    t = np.arange(HG_GROUP * HG_CHUNK)
    same_chunk = (t[None, :] // HG_CHUNK) == (t[:, None] // HG_CHUNK)
    lower = (t[None, :] <= t[:, None]) & same_chunk
    upper = (t[None, :] >= t[:, None]) & same_chunk
    return jnp.asarray(np.stack([lower, upper]).astype(np.float32), dtype=BF16)


def _split3_cols(x):
    hi = x.astype(BF16)
    r = x - hi.astype(F32)
    mid = r.astype(BF16)
    lo = (r - mid.astype(F32)).astype(BF16)
    return jnp.concatenate([hi, mid, lo], axis=1)


def _hgrn_prefix(q, fpre, v, lb, tri, reverse):
    rows, w = q.shape
    c = HG_CHUNK
    f = jnp.maximum(lb + (1.0 - lb) * jax.nn.sigmoid(fpre), F_MIN)
    lf = jnp.log(f)
    kk = 1.0 - f
    r = _dot(tri, _split3_cols(lf))
    b = r[:, 2 * w:] + r[:, w:2 * w] + r[:, :w]
    b3 = b.reshape(rows // HG_BLK, HG_BLK, w)
    bmid = jnp.broadcast_to(b3[:, HG_MID:HG_MID + 1, :], b3.shape).reshape(rows, w)
    blast = [b[g * c:g * c + 1] if reverse else b[(g + 1) * c - 1:(g + 1) * c] for g in range(rows // c)]
    blast_rows = jnp.concatenate([jnp.broadcast_to(x, (c, w)) for x in blast], axis=0)
    return dict(q=q, b=b, blast=blast, vb=v.astype(BF16), reverse=reverse,
                qe=(q * jnp.exp(b)).astype(BF16),
                kdec=(kk * jnp.exp(blast_rows - b)).astype(BF16),
                ktil=kk * jnp.exp(bmid - b))


def _hgrn_tilde(pre, rs, sl, blk_mask):
    qh, bh, reverse = pre["q"][rs, sl], pre["b"][rs, sl], pre["reverse"]
    c = qh.shape[0]
    nblk = c // HG_BLK
    pieces = []
    for j in range(nblk):
        ref_row = bh[j * HG_BLK + HG_MID:j * HG_BLK + HG_MID + 1, :]
        if reverse:
            hi_row = (j + 1) * HG_BLK
            qt = qh[:hi_row] * jnp.exp(bh[:hi_row] - ref_row)
            if hi_row < c:
                qt = jnp.concatenate([qt, jnp.zeros((c - hi_row, HEAD_DIM), F32)], axis=0)
        else:
            lo_row = j * HG_BLK
            qt = qh[lo_row:] * jnp.exp(bh[lo_row:] - ref_row)
            if lo_row > 0:
                qt = jnp.concatenate([jnp.zeros((lo_row, HEAD_DIM), F32), qt], axis=0)
        pieces.append(qt)
    qtil = jnp.concatenate(pieces, axis=1).astype(BF16)
    kt = jnp.concatenate([pre["ktil"][rs, sl]] * nblk, axis=1)
    return qtil, jnp.where(blk_mask, kt, 0.0).astype(BF16)


def _hgrn_kernel(qf_ref, ff_ref, vf_ref, qb_ref, fb_ref, vb_ref, lbf_ref, lbb_ref, tri_ref,
                 of_ref, ob_ref, sf_ref, sb_ref):
    @pl.when(pl.program_id(1) == 0)
    def _():
        sf_ref[...] = jnp.zeros_like(sf_ref)
        sb_ref[...] = jnp.zeros_like(sb_ref)

    c = HG_CHUNK
    ngrp = qf_ref.shape[0] // c
    nblk = c // HG_BLK
    row = lax.broadcasted_iota(jnp.int32, (c, c), 0)
    col = lax.broadcasted_iota(jnp.int32, (c, c), 1)
    brow = lax.broadcasted_iota(jnp.int32, (c, nblk * HEAD_DIM), 0) // HG_BLK
    bcol = lax.broadcasted_iota(jnp.int32, (c, nblk * HEAD_DIM), 1) // HEAD_DIM
    blk_mask = brow == bcol

    pres = [_hgrn_prefix(qf_ref[...], ff_ref[...], vf_ref[...], lbf_ref[...], tri_ref[0], False),
            _hgrn_prefix(qb_ref[...], fb_ref[...], vb_ref[...], lbb_ref[...], tri_ref[1], True)]
    order = [list(range(ngrp)), list(range(ngrp - 1, -1, -1))]
    units = [(d, g, h) for d in range(2) for g in order[d] for h in range(B_HEADS)]
    sl = lambda h: slice(h * HEAD_DIM, (h + 1) * HEAD_DIM)
    rs = lambda g: slice(g * c, (g + 1) * c)
    state_refs = (sf_ref, sb_ref)

    upd = {u: _dot_tn(pres[u[0]]["vb"][rs(u[1]), sl(u[2])], pres[u[0]]["kdec"][rs(u[1]), sl(u[2])])
           for u in units}
    attn = {}
    for d, g, h in units:
        qtil, kt = _hgrn_tilde(pres[d], rs(g), sl(h), blk_mask)
        attn[d, g, h] = _dot_nt(qtil, kt)
    inter = {}
    for d in range(2):
        for h in range(B_HEADS):
            st = state_refs[d][h]
            for g in order[d]:
                inter[d, g, h] = _dot_nt(pres[d]["qe"][rs(g), sl(h)], st.astype(BF16))
                st = st * jnp.exp(pres[d]["blast"][g][:, sl(h)]) + upd[d, g, h]
            state_refs[d][h] = st
    outs = [[[None] * B_HEADS for _ in range(ngrp)] for _ in range(2)]
    for d, g, h in units:
        keep = (col >= row) if pres[d]["reverse"] else (col <= row)
        a = jnp.where(keep, attn[d, g, h], 0.0).astype(BF16)
        outs[d][g][h] = _dot(a, pres[d]["vb"][rs(g), sl(h)]) + inter[d, g, h]
    for d, o_ref in enumerate((of_ref, ob_ref)):
        o_ref[...] = jnp.concatenate([jnp.concatenate(per_head, axis=1) for per_head in outs[d]], axis=0)


def _hgrn(bmix, lbf, lbb, tri, seq):
    t = bmix.shape[0]
    bsz = t // seq
    c = HG_GROUP * HG_CHUNK
    n = seq // c
    fwd = lambda k: pl.BlockSpec((c, B_W), lambda b, j: (b * n + j, k))
    bwd = lambda k: pl.BlockSpec((c, B_W), lambda b, j: (b * n + n - 1 - j, k))
    const2 = lambda b, j: (0, 0)
    return pl.pallas_call(
        _hgrn_kernel,
        grid=(bsz, n),
        in_specs=[fwd(0), fwd(1), fwd(3), bwd(0), bwd(2), bwd(3),
                  pl.BlockSpec((1, B_W), const2), pl.BlockSpec((1, B_W), const2),
                  pl.BlockSpec(tri.shape, lambda b, j: (0, 0, 0))],
        out_specs=[fwd(0), bwd(0)],
        out_shape=[jax.ShapeDtypeStruct((t, B_W), F32), jax.ShapeDtypeStruct((t, B_W), F32)],
        scratch_shapes=[pltpu.VMEM((B_HEADS, HEAD_DIM, HEAD_DIM), F32)] * 2,
        compiler_params=_cparams("arbitrary", "arbitrary"),
        name="hgrn",
    )(bmix, bmix, bmix, bmix, bmix, bmix, lbf, lbb, tri)


def _out_proj_kernel(x_ref, oa_ref, of_ref, ob_ref, bg_ref, oc_ref, w_ref, gb_ref, bd_ref,
                     g1_ref, sh_ref, sc_ref, g_ref, xo_ref, h_ref, wb_ref):
    _cast_weights_once(w_ref, wb_ref)
    ob = of_ref[...] + ob_ref[...]
    ob = ob * lax.rsqrt(_head_mean_sq(ob, bd_ref[...]) + EPS) * gb_ref[...]
    ob = (ob * _silu(bg_ref[...])).astype(BF16)
    mix = (_dot(oa_ref[...], wb_ref[:A_W, :]) + _dot(ob, wb_ref[A_W:A_W + B_W, :])
           + _dot(oc_ref[...], wb_ref[A_W + B_W:, :]))
    x = x_ref[...] + g1_ref[...] * mix
    xo_ref[...] = x
    y = x * lax.rsqrt(jnp.mean(x * x, axis=-1, keepdims=True) + EPS) * g_ref[...]
    h_ref[...] = (y * (1.0 + sc_ref[...]) + sh_ref[...]).astype(BF16)


def _out_proj(x2, oa, of, ob, bmix, oc, w_all, layer, gb, bd, mod3, g, seq):
    t, d = x2.shape
    tm = TM_PROJ
    per_seq = seq // tm
    row = lambda i: (i, 0)
    const = lambda i: (0, 0)
    modspec = lambda k: pl.BlockSpec((None, 1, d), lambda i: (i // per_seq, 0, k))
    return pl.pallas_call(
        _out_proj_kernel,
        grid=(t // tm,),
        in_specs=[
            pl.BlockSpec((tm, d), row),
            pl.BlockSpec((tm, A_W), row),
            pl.BlockSpec((tm, B_W), row), pl.BlockSpec((tm, B_W), row),
            pl.BlockSpec((tm, B_W), lambda i: (i, 4)),
            pl.BlockSpec((tm, C_W), row),
            _layer_weight_spec(w_all, layer),
            pl.BlockSpec((1, B_W), const),
            pl.BlockSpec(bd.shape, const),
            modspec(2), modspec(3), modspec(4),
            pl.BlockSpec((1, d), const),
        ],
        out_specs=[pl.BlockSpec((tm, d), row), pl.BlockSpec((tm, d), row)],
        out_shape=[jax.ShapeDtypeStruct((t, d), F32), jax.ShapeDtypeStruct((t, d), BF16)],
        scratch_shapes=[pltpu.VMEM(w_all.shape[1:], BF16)],
        compiler_params=_cparams("arbitrary"),
        name="out_proj",
    )(x2, oa, of, ob, bmix, oc, w_all, gb, bd, mod3, mod3, mod3, g)


def _ffn_up_kernel(h_ref, wa_ref, wb_ref, cwa_ref, cwb_ref, cba_ref, cbb_ref, o_ref):
    h = h_ref[...]
    seq = h.shape[0]
    tn = o_ref.shape[1]
    row = lax.broadcasted_iota(jnp.int32, (seq, 1), 0)
    subs = [(off, min(MXU_W, tn - off)) for off in range(0, tn, MXU_W)]

    def matmuls(sub):
        off, w = sub
        if w == MXU_W:
            return _dot(h, wa_ref[:, off:off + w]), _dot(h, wb_ref[:, off:off + w])
        u = _dot(h, jnp.concatenate([wa_ref[:, off:off + w], wb_ref[:, off:off + w]], axis=1))
        return u[:, :w], u[:, w:]

    def conv(u, cw, cb):
        prev = jnp.where(row == 0, 0.0, pltpu.roll(u, 1, 0))
        nxt = jnp.where(row == seq - 1, 0.0, pltpu.roll(u, seq - 1, 0))
        return cw[0:1, :] * prev + cw[1:2, :] * u + cw[2:3, :] * nxt + cb

    def gate(sub, us):
        off, w = sub
        cols = slice(off, off + w)
        a = conv(us[0], cwa_ref[:, cols], cba_ref[:, cols])
        b = conv(us[1], cwb_ref[:, cols], cbb_ref[:, cols])
        o_ref[:, cols] = (_silu(a) * b).astype(BF16)

    _run_ahead(subs, matmuls, gate, ahead=2)


def _ffn_up(h, w_up, conv_w, conv_b, layer, seq):
    t, d = h.shape
    dff = w_up.shape[2] // 2
    tn = TN_FFN
    nt = dff // tn
    lo = lambda b, j: (layer, 0, j)
    hi = lambda b, j: (layer, 0, nt + j)
    return pl.pallas_call(
        _ffn_up_kernel,
        grid=(t // seq, nt),
        in_specs=[
            pl.BlockSpec((seq, d), lambda b, j: (b, 0)),
            pl.BlockSpec((None, d, tn), lo), pl.BlockSpec((None, d, tn), hi),
            pl.BlockSpec((None, CONV_W, tn), lo), pl.BlockSpec((None, CONV_W, tn), hi),
            pl.BlockSpec((None, 1, tn), lo), pl.BlockSpec((None, 1, tn), hi),
        ],
        out_specs=pl.BlockSpec((seq, tn), lambda b, j: (b, j)),
        out_shape=jax.ShapeDtypeStruct((t, dff), BF16),
        compiler_params=_cparams("arbitrary", "arbitrary"),
        name="ffn_up",
    )(h, w_up, w_up, conv_w, conv_w, conv_b, conv_b)


def _ffn_down_kernel(x_ref, a_ref, w_ref, g2_ref, o_ref, wb_ref):
    _cast_weights_once(w_ref, wb_ref)
    o_ref[...] = x_ref[...] + g2_ref[...] * _dot(a_ref[...], wb_ref[...])


def _ffn_down(x2, act, w_all, layer, mod3, seq):
    t, d = x2.shape
    tm = TM_PROJ
    per_seq = seq // tm
    row = lambda i: (i, 0)
    return pl.pallas_call(
        _ffn_down_kernel,
        grid=(t // tm,),
        in_specs=[
            pl.BlockSpec((tm, d), row),
            pl.BlockSpec((tm, act.shape[1]), row),
            _layer_weight_spec(w_all, layer),
            pl.BlockSpec((None, 1, d), lambda i: (i // per_seq, 0, 5)),
        ],
        out_specs=pl.BlockSpec((tm, d), row),
        out_shape=jax.ShapeDtypeStruct((t, d), F32),
        scratch_shapes=[pltpu.VMEM(w_all.shape[1:], BF16)],
        compiler_params=_cparams("arbitrary"),
        name="ffn_down",
    )(x2, act, w_all, mod3)


def _rope_tables(seq):
    n_rows = seq // GRID_W
    rowp = np.repeat(np.arange(n_rows), GRID_W).astype(np.float32)
    colp = np.tile(np.arange(GRID_W), n_rows).astype(np.float32)
    half = HEAD_DIM // 2
    inv = (np.float32(ROPE_THETA) ** (-np.arange(0, half, 2, dtype=np.float32) / half)).astype(np.float32)
    ang_r = rowp[:, None] * inv
    ang_c = colp[:, None] * inv
    cos = np.concatenate([np.cos(ang_r), np.cos(ang_r), np.cos(ang_c), np.cos(ang_c)], axis=1)
    sin = np.concatenate([-np.sin(ang_r), np.sin(ang_r), -np.sin(ang_c), np.sin(ang_c)], axis=1)
    tile = lambda a: jnp.asarray(np.tile(a.astype(np.float32), (1, A_HEADS)))
    return tile(cos), tile(sin)


def _head_block_ones(width):
    i = np.arange(width) // HEAD_DIM
    return jnp.asarray((i[:, None] == i[None, :]).astype(np.float32), dtype=BF16)


def kernel(x, c, w_ada, b_ada, norm_g, w_in, a_q_norm, a_k_norm, b_lb, b_out_norm, c_q_norm, c_k_norm,
           w_out, w_up, conv_w, conv_b, w_down):
    bsz, seq, d = x.shape
    depth = w_in.shape[0]
    t = bsz * seq
    scale = HEAD_DIM ** -0.5

    mod = _ada(c, w_ada, b_ada)
    lb_all = _hgrn_lb(b_lb.astype(F32))
    cos, sin = _rope_tables(seq)
    bd = _head_block_ones(A_W)
    tri = _hgrn_tri_constants()
    bias_strided, bias_local = _dilated_bias_tables(seq)
    tile = lambda gvec, heads, s=1.0: (jnp.tile(gvec.astype(F32), heads) * s).reshape(1, -1)
    w_up_bf16 = w_up.astype(BF16)
    conv_b3 = conv_b.reshape(depth, 1, -1)

    x2 = x.reshape(t, d)
    for l in range(depth):
        mod3 = mod[l].reshape(bsz, 1, 6 * d)
        aqt, ak, avt, bmix, cq, ck, cv = _in_proj(
            x2, mod3, norm_g[l, 0].reshape(1, d), w_in, l, cos, sin,
            tile(a_q_norm[l], A_HEADS, scale), tile(a_k_norm[l], A_KV_HEADS),
            tile(c_q_norm[l], C_HEADS, scale), tile(c_k_norm[l], C_KV_HEADS), bd, seq)
        o_a = _attention(aqt, ak, avt, seq, n_heads=A_HEADS, n_kv=A_KV_HEADS)
        far = _dil_strided(cq, ck, cv, bias_strided, seq)
        o_c = _dil_local(cq, ck, cv, bias_local, far, seq)
        o_f, o_b = _hgrn(bmix, lb_all[0, l].reshape(1, B_W), lb_all[1, l].reshape(1, B_W), tri, seq)
        x2, h2 = _out_proj(x2, o_a, o_f, o_b, bmix, o_c, w_out, l,
                           tile(b_out_norm[l], B_HEADS), bd[:B_W, :B_W], mod3,
                           norm_g[l, 1].reshape(1, d), seq)
        act = _ffn_up(h2, w_up_bf16, conv_w, conv_b3, l, seq)
        x2 = _ffn_down(x2, act, w_down, l, mod3, seq)
    return x2.reshape(bsz, seq, d)
```
